```python
import jax, jax.numpy as jnp
from jax import lax
import numpy as np

D_MODEL = 1024
BATCH = 8
SEQ = 2048
DEPTH = 4
DEC_BATCH = 32
DEC_SEQ = 16
PAST_LEN = 1024

CHUNK = 64
N_HEADS = 16
HEAD_DIM = D_MODEL // N_HEADS
D_FF = 2816
LEFT_CHUNKS = 8
BAND = (LEFT_CHUNKS + 1) * CHUNK
A_WINDOW = LEFT_CHUNKS * CHUNK
A_CACHE = min(A_WINDOW, PAST_LEN)
A_KEEP = min(A_WINDOW, SEQ)
REL_MAX = 256
N_REL = 2 * REL_MAX + 1
Q_BLOCK = 128
N_MIXERS = 3
N_A = (DEPTH + 2) // 3
N_B = (DEPTH + 1) // 3
N_C = DEPTH // 3
RMS_EPS = 1e-6
ATTN_SCALE = HEAD_DIM ** -0.5
NEG_INF = -1e30
FORGET_BIAS_INIT = 3.0

kernel_name = 'hybrid_streaming_chunk_encoder_step'


def rms_norm(x, g):
    xf = x.astype(jnp.float32)
    y = xf * lax.rsqrt(jnp.mean(xf * xf, axis=-1, keepdims=True) + RMS_EPS)
    return (y * g.astype(jnp.float32)).astype(x.dtype)


def swiglu(h, w_gate, w_up, w_down):
    return (jax.nn.silu(h @ w_gate) * (h @ w_up)) @ w_down


def project_qkv(h, w_qkv):
    q, k, v = jnp.split(h @ w_qkv, 3, axis=-1)
    shape = h.shape[:-1] + (N_HEADS, HEAD_DIM)
    return q.reshape(shape), k.reshape(shape), v.reshape(shape)


def rel_bias(rel_table, dist):
    idx = jnp.clip(dist, -REL_MAX, REL_MAX) + REL_MAX
    return jnp.moveaxis(rel_table[idx].astype(jnp.float32), -1, 0)


def chunk_attn_prompt(q, k, v, rel_table):
    b, s = q.shape[:2]
    nc = s // CHUNK
    qc = q.reshape(b, nc, CHUNK, N_HEADS, HEAD_DIM)
    pad = ((0, 0), (LEFT_CHUNKS * CHUNK, 0), (0, 0), (0, 0))
    kp = jnp.pad(k, pad).reshape(b, nc + LEFT_CHUNKS, CHUNK, N_HEADS, HEAD_DIM)
    vp = jnp.pad(v, pad).reshape(b, nc + LEFT_CHUNKS, CHUNK, N_HEADS, HEAD_DIM)
    kb = jnp.concatenate([kp[:, o:o + nc] for o in range(LEFT_CHUNKS + 1)], axis=2)
    vb = jnp.concatenate([vp[:, o:o + nc] for o in range(LEFT_CHUNKS + 1)], axis=2)
    i = jnp.arange(CHUNK)[:, None]
    m = jnp.arange(BAND)[None, :]
    bias = rel_bias(rel_table, A_WINDOW + i - m)
    valid = (jnp.arange(nc)[:, None] - LEFT_CHUNKS + jnp.arange(BAND)[None, :] // CHUNK) >= 0
    sc = jnp.einsum('bcihd,bcmhd->bhcim', qc, kb).astype(jnp.float32) * ATTN_SCALE
    sc = sc + bias[None, :, None, :, :]
    sc = jnp.where(valid[None, None, :, None, :], sc, NEG_INF)
    p = jax.nn.softmax(sc, axis=-1).astype(v.dtype)
    o = jnp.einsum('bhcim,bcmhd->bcihd', p, vb)
    return o.reshape(b, s, N_HEADS, HEAD_DIM)


def chunk_attn_sample(q, k, v, cache_k, cache_v, rel_table):
    n_cache = cache_k.shape[1]
    L = q.shape[1]
    kk = jnp.concatenate([cache_k, k], axis=1)
    vv = jnp.concatenate([cache_v, v], axis=1)
    i = jnp.arange(L)[:, None]
    m = jnp.arange(n_cache + L)[None, :]
    bias = rel_bias(rel_table, n_cache + i - m)
    sc = jnp.einsum('bihd,bmhd->bhim', q, kk).astype(jnp.float32) * ATTN_SCALE + bias[None]
    p = jax.nn.softmax(sc, axis=-1).astype(v.dtype)
    return jnp.einsum('bhim,bmhd->bihd', p, vv)


def _fox_block(q_blk, cum_q, q_pos, k, v, cum_k, k_pos):
    sc = jnp.einsum('bqhd,bkhd->bhqk', q_blk, k).astype(jnp.float32) * ATTN_SCALE
    sc = sc + jnp.swapaxes(cum_q, 1, 2)[..., :, None] - jnp.swapaxes(cum_k, 1, 2)[..., None, :]
    sc = jnp.where(k_pos[None, :] <= q_pos[:, None], sc, NEG_INF)
    p = jax.nn.softmax(sc, axis=-1).astype(v.dtype)
    return jnp.einsum('bhqk,bkhd->bqhd', p, v)


def fox_prompt(q, k, v, log_f):
    b, s = q.shape[:2]
    nb = s // Q_BLOCK
    cum = jnp.cumsum(log_f, axis=1)
    pos = jnp.arange(s)
    qb = jnp.swapaxes(q.reshape(b, nb, Q_BLOCK, N_HEADS, HEAD_DIM), 0, 1)
    cb = jnp.swapaxes(cum.reshape(b, nb, Q_BLOCK, N_HEADS), 0, 1)
    pb = pos.reshape(nb, Q_BLOCK)
    out = lax.map(lambda a: _fox_block(a[0], a[1], a[2], k, v, cum, pos), (qb, cb, pb))
    return jnp.swapaxes(out, 0, 1).reshape(b, s, N_HEADS, HEAD_DIM)


def fox_sample(q, k, v, log_f, cache_k, cache_v, cache_log_f):
    past = cache_k.shape[1]
    L = q.shape[1]
    kk = jnp.concatenate([cache_k, k], axis=1)
    vv = jnp.concatenate([cache_v, v], axis=1)
    cum = jnp.cumsum(jnp.concatenate([cache_log_f.astype(jnp.float32), log_f], axis=1), axis=1)
    return _fox_block(q, cum[:, past:], past + jnp.arange(L), kk, vv, cum, jnp.arange(past + L))


def _stick_block(q_blk, q_pos, k, v, k_pos):
    z = jnp.einsum('bqhd,bkhd->bhqk', q_blk, k).astype(jnp.float32) * ATTN_SCALE
    earlier = k_pos[None, :] < q_pos[:, None]
    log_beta = jax.nn.log_sigmoid(z)
    log_keep = jnp.where(earlier, log_beta - z, 0.0)
    tail = lax.cumsum(log_keep, axis=3, reverse=True) - log_keep
    w = jnp.where(earlier, jnp.exp(log_beta + tail), 0.0)
    return jnp.einsum('bhqk,bkhd->bqhd', w.astype(v.dtype), v)


def stick_prompt(q, k, v):
    b, s = q.shape[:2]
    nb = s // Q_BLOCK
    pos = jnp.arange(s)
    qb = jnp.swapaxes(q.reshape(b, nb, Q_BLOCK, N_HEADS, HEAD_DIM), 0, 1)
    pb = pos.reshape(nb, Q_BLOCK)
    out = lax.map(lambda a: _stick_block(a[0], a[1], k, v, pos), (qb, pb))
    return jnp.swapaxes(out, 0, 1).reshape(b, s, N_HEADS, HEAD_DIM)


def stick_sample(q, k, v, cache_k, cache_v):
    past = cache_k.shape[1]
    L = q.shape[1]
    kk = jnp.concatenate([cache_k, k], axis=1)
    vv = jnp.concatenate([cache_v, v], axis=1)
    return _stick_block(q, past + jnp.arange(L), kk, vv, jnp.arange(past + L))


def setup_inputs(seed: int = 0) -> dict:
    key = jax.random.key(seed)
    ks = jax.random.split(key, 32)
    d = D_MODEL

    def nrm(k, shape, scale):
        return jax.random.normal(k, shape, jnp.float32) * scale

    def gain(k, shape):
        return 1.0 + 0.05 * jax.random.normal(k, shape, jnp.float32)

    return {
        'x_prompt': nrm(ks[0], (BATCH, SEQ, d), 1.0),
        'x_sample': nrm(ks[1], (DEC_BATCH, DEC_SEQ, d), 1.0),
        'cache_a_k': nrm(ks[2], (N_A, DEC_BATCH, A_CACHE, N_HEADS, HEAD_DIM), 1.0),
        'cache_a_v': nrm(ks[3], (N_A, DEC_BATCH, A_CACHE, N_HEADS, HEAD_DIM), 1.0),
        'cache_b_k': nrm(ks[4], (N_B, DEC_BATCH, PAST_LEN, N_HEADS, HEAD_DIM), 1.0),
        'cache_b_v': nrm(ks[5], (N_B, DEC_BATCH, PAST_LEN, N_HEADS, HEAD_DIM), 1.0),
        'cache_b_logf': jax.nn.log_sigmoid(FORGET_BIAS_INIT + nrm(ks[6], (N_B, DEC_BATCH, PAST_LEN, N_HEADS), 1.0)),
        'cache_c_k': nrm(ks[7], (N_C, DEC_BATCH, PAST_LEN, N_HEADS, HEAD_DIM), 1.0),
        'cache_c_v': nrm(ks[8], (N_C, DEC_BATCH, PAST_LEN, N_HEADS, HEAD_DIM), 1.0),
        'norm_ffn1': gain(ks[9], (DEPTH, d)),
        'ffn1_gate': nrm(ks[10], (DEPTH, d, D_FF), d ** -0.5),
        'ffn1_up': nrm(ks[11], (DEPTH, d, D_FF), d ** -0.5),
        'ffn1_down': nrm(ks[12], (DEPTH, D_FF, d), D_FF ** -0.5),
        'norm_mix': gain(ks[13], (DEPTH, d)),
        'norm_ffn2': gain(ks[14], (DEPTH, d)),
        'ffn2_gate': nrm(ks[15], (DEPTH, d, D_FF), d ** -0.5),
        'ffn2_up': nrm(ks[16], (DEPTH, d, D_FF), d ** -0.5),
        'ffn2_down': nrm(ks[17], (DEPTH, D_FF, d), D_FF ** -0.5),
        'a_w_qkv': nrm(ks[18], (N_A, d, 3 * d), d ** -0.5),
        'a_w_o': nrm(ks[19], (N_A, d, d), d ** -0.5),
        'a_rel_bias': nrm(ks[20], (N_A, N_REL, N_HEADS), 0.5),
        'b_w_qkv': nrm(ks[21], (N_B, d, 3 * d), d ** -0.5),
        'b_w_o': nrm(ks[22], (N_B, d, d), d ** -0.5),
        'b_w_f': nrm(ks[23], (N_B, d, N_HEADS), d ** -0.5),
        'b_b_f': FORGET_BIAS_INIT + nrm(ks[24], (N_B, N_HEADS), 0.1),
        'c_w_qkv': nrm(ks[25], (N_C, d, 3 * d), d ** -0.5),
        'c_w_o': nrm(ks[26], (N_C, d, d), d ** -0.5),
        'norm_final': gain(ks[27], (d,)),
    }


def reference(x_prompt, x_sample, cache_a_k, cache_a_v, cache_b_k, cache_b_v, cache_b_logf,
              cache_c_k, cache_c_v, norm_ffn1, ffn1_gate, ffn1_up, ffn1_down, norm_mix,
              norm_ffn2, ffn2_gate, ffn2_up, ffn2_down, a_w_qkv, a_w_o, a_rel_bias,
              b_w_qkv, b_w_o, b_w_f, b_b_f, c_w_qkv, c_w_o, norm_final):
    xp, xs = x_prompt, x_sample
    a_kp, a_vp, a_ks, a_vs = [], [], [], []
    b_kp, b_vp, b_fp, b_ks, b_vs, b_fs = [], [], [], [], [], []
    c_kp, c_vp, c_ks, c_vs = [], [], [], []
    for i in range(DEPTH):
        kind, slot = i % N_MIXERS, i // N_MIXERS
        xp = xp + 0.5 * swiglu(rms_norm(xp, norm_ffn1[i]), ffn1_gate[i], ffn1_up[i], ffn1_down[i])
        xs = xs + 0.5 * swiglu(rms_norm(xs, norm_ffn1[i]), ffn1_gate[i], ffn1_up[i], ffn1_down[i])
        hp = rms_norm(xp, norm_mix[i])
        hs = rms_norm(xs, norm_mix[i])
        if kind == 0:
            qp, kp, vp = project_qkv(hp, a_w_qkv[slot])
            qs, ks_, vs = project_qkv(hs, a_w_qkv[slot])
            op = chunk_attn_prompt(qp, kp, vp, a_rel_bias[slot])
            os_ = chunk_attn_sample(qs, ks_, vs, cache_a_k[slot], cache_a_v[slot], a_rel_bias[slot])
            a_kp.append(kp[:, kp.shape[1] - A_KEEP:])
            a_vp.append(vp[:, vp.shape[1] - A_KEEP:])
            a_ks.append(ks_)
            a_vs.append(vs)
            w_o = a_w_o[slot]
        elif kind == 1:
            qp, kp, vp = project_qkv(hp, b_w_qkv[slot])
            qs, ks_, vs = project_qkv(hs, b_w_qkv[slot])
            fp = jax.nn.log_sigmoid((hp @ b_w_f[slot] + b_b_f[slot]).astype(jnp.float32))
            fs = jax.nn.log_sigmoid((hs @ b_w_f[slot] + b_b_f[slot]).astype(jnp.float32))
            op = fox_prompt(qp, kp, vp, fp)
            os_ = fox_sample(qs, ks_, vs, fs, cache_b_k[slot], cache_b_v[slot], cache_b_logf[slot])
            b_kp.append(kp)
            b_vp.append(vp)
            b_fp.append(fp)
            b_ks.append(ks_)
            b_vs.append(vs)
            b_fs.append(fs)
            w_o = b_w_o[slot]
        else:
            qp, kp, vp = project_qkv(hp, c_w_qkv[slot])
            qs, ks_, vs = project_qkv(hs, c_w_qkv[slot])
            op = stick_prompt(qp, kp, vp)
            os_ = stick_sample(qs, ks_, vs, cache_c_k[slot], cache_c_v[slot])
            c_kp.append(kp)
            c_vp.append(vp)
            c_ks.append(ks_)
            c_vs.append(vs)
            w_o = c_w_o[slot]
        xp = xp + op.reshape(xp.shape) @ w_o
        xs = xs + os_.reshape(xs.shape) @ w_o
        xp = xp + 0.5 * swiglu(rms_norm(xp, norm_ffn2[i]), ffn2_gate[i], ffn2_up[i], ffn2_down[i])
        xs = xs + 0.5 * swiglu(rms_norm(xs, norm_ffn2[i]), ffn2_gate[i], ffn2_up[i], ffn2_down[i])
    y_prompt = rms_norm(xp, norm_final)
    y_sample = rms_norm(xs, norm_final)
    return (y_prompt, y_sample,
            jnp.stack(a_kp), jnp.stack(a_vp), jnp.stack(a_ks), jnp.stack(a_vs),
            jnp.stack(b_kp), jnp.stack(b_vp), jnp.stack(b_fp),
            jnp.stack(b_ks), jnp.stack(b_vs), jnp.stack(b_fs),
            jnp.stack(c_kp), jnp.stack(c_vp), jnp.stack(c_ks), jnp.stack(c_vs))
```

```python
import functools

import jax
import jax.numpy as jnp
from jax import lax
from jax.experimental import pallas as pl
from jax.experimental.pallas import tpu as pltpu

F32 = jnp.float32
BF16 = jnp.bfloat16

N_HEADS = 16
HEAD_DIM = 64
CHUNK = 64
LEFT_CHUNKS = 8
A_WINDOW = LEFT_CHUNKS * CHUNK
REL_MAX = 256
RMS_EPS = 1e-6
ATTN_SCALE = HEAD_DIM ** -0.5
NEG_INF = -1e30

LANE = 128
HEADS_PER_STEP = LANE // HEAD_DIM
VMEM_LIMIT = 56 * 1024 * 1024

ROW_TILE = 512
FF_TILE = 256
Q_TILE = 256
K_TILE = 256


def _params(*sem):
    return pltpu.CompilerParams(dimension_semantics=sem, vmem_limit_bytes=VMEM_LIMIT)


def _rms(x, g):
    y = x * lax.rsqrt(jnp.mean(x * x, axis=-1, keepdims=True) + RMS_EPS)
    return y * g


def _log_sigmoid(z):
    return jnp.minimum(z, 0.0) - jnp.log1p(jnp.exp(-jnp.abs(z)))


def _dot(a, b):
    return jnp.dot(a, b, preferred_element_type=F32)


def _dot_nt(a, b):
    return lax.dot_general(a, b, (((1,), (1,)), ((), ())), preferred_element_type=F32)


def _dot_tn(a, b):
    return lax.dot_general(a, b, (((0,), (0,)), ((), ())), preferred_element_type=F32)


def _split_bf16(x):
    hi = x.astype(BF16)
    lo = (x - hi.astype(F32)).astype(BF16)
    return hi, lo


def _resident(shape):
    return pl.BlockSpec(shape, lambda *_: (0,) * len(shape), pipeline_mode=pl.Buffered(1))


def _ffn_kernel(*refs, has_proj, has_final):
    refs = list(refs)
    x_ref = refs.pop(0)
    if has_proj:
        o_ref, wo_ref = refs.pop(0), refs.pop(0)
    g_ref, wg_ref, wu_ref, wd_ref = refs[:4]
    refs = refs[4:]
    if has_final:
        gf_ref = refs.pop(0)
    out_ref = refs.pop(0)

    x = x_ref[...]
    if has_proj:
        x = x + _dot(o_ref[...], wo_ref[...])
    h = _rms(x, g_ref[...]).astype(BF16)
    d_ff = wg_ref.shape[1]
    acc = jnp.zeros(x.shape, F32)
    for c in range(d_ff // FF_TILE):
        sl = slice(c * FF_TILE, (c + 1) * FF_TILE)
        gate = _dot(h, wg_ref[:, sl])
        up = _dot(h, wu_ref[:, sl])
        act = (gate * jax.nn.sigmoid(gate) * up).astype(BF16)
        acc = acc + _dot(act, wd_ref[sl, :])
    y = x + 0.5 * acc
    if has_final:
        y = _rms(y, gf_ref[...])
    out_ref[...] = y


def _ffn(x, g, wg, wu, wd, proj=None, final_g=None):
    n, d = x.shape
    d_ff = wg.shape[1]
    tm = min(ROW_TILE, n)
    assert n % tm == 0 and d_ff % FF_TILE == 0
    row = pl.BlockSpec((tm, d), lambda i: (i, 0))
    args, specs = [x], [row]
    if proj is not None:
        args += list(proj)
        specs += [row, _resident((d, d))]
    args += [g.reshape(1, d), wg, wu, wd]
    specs += [_resident((1, d)), _resident((d, d_ff)), _resident((d, d_ff)), _resident((d_ff, d))]
    if final_g is not None:
        args.append(final_g.reshape(1, d))
        specs.append(_resident((1, d)))
    return pl.pallas_call(
        functools.partial(_ffn_kernel, has_proj=proj is not None, has_final=final_g is not None),
        grid=(n // tm,),
        in_specs=specs,
        out_specs=row,
        out_shape=jax.ShapeDtypeStruct((n, d), F32),
        compiler_params=_params("parallel"),
        name="ffn",
    )(*args)


def _qkv_kernel(*refs, has_forget):
    x_ref, g_ref, w_ref = refs[:3]
    if has_forget:
        wf_ref, bf_ref = refs[3:5]
        q_ref, k_ref, v_ref, kb_ref, vb_ref, f_ref = refs[5:]
    else:
        q_ref, k_ref, v_ref, kb_ref, vb_ref = refs[3:]
    d = x_ref.shape[1]
    h = _rms(x_ref[...], g_ref[...]).astype(BF16)
    q_ref[...] = (_dot(h, w_ref[:, :d]) * ATTN_SCALE).astype(BF16)
    k = _dot(h, w_ref[:, d:2 * d])
    k_ref[...] = k
    kb_ref[...] = k.astype(BF16)
    v = _dot(h, w_ref[:, 2 * d:])
    v_ref[...] = v
    vb_ref[...] = v.astype(BF16)
    if has_forget:
        f_ref[...] = _log_sigmoid(_dot(h, wf_ref[...]) + bf_ref[...])


def _qkv(x, g, w, forget=None):
    n, d = x.shape
    tm = min(ROW_TILE, n)
    assert n % tm == 0
    row = pl.BlockSpec((tm, d), lambda i: (i, 0))
    args = [x, g.reshape(1, d), w]
    specs = [row, _resident((1, d)), _resident((d, 3 * d))]
    shapes = [jax.ShapeDtypeStruct((n, d), BF16), jax.ShapeDtypeStruct((n, d), F32),
              jax.ShapeDtypeStruct((n, d), F32), jax.ShapeDtypeStruct((n, d), BF16),
              jax.ShapeDtypeStruct((n, d), BF16)]
    out_specs = [row] * 5
    if forget is not None:
        args += list(forget)
        specs += [_resident((d, LANE)), _resident((1, LANE))]
        shapes.append(jax.ShapeDtypeStruct((n, LANE), F32))
        out_specs.append(pl.BlockSpec((tm, LANE), lambda i: (i, 0)))
    return pl.pallas_call(
        functools.partial(_qkv_kernel, has_forget=forget is not None),
        grid=(n // tm,),
        in_specs=specs,
        out_specs=out_specs,
        out_shape=shapes,
        compiler_params=_params("parallel"),
        name="qkv",
    )(*args)


def _prompt_specs(s, b_first=True):
    if b_first:
        q_map = lambda b, hp, qi: (b * (s // Q_TILE) + qi, hp)
        kv_map = lambda b, hp, qi: (b, hp)
    else:
        q_map = lambda hp, b, qi: (b * (s // Q_TILE) + qi, hp)
        kv_map = lambda hp, b, qi: (b, hp)
    return pl.BlockSpec((Q_TILE, LANE), q_map), pl.BlockSpec((s, LANE), kv_map)


def _tile_iota():
    row = lax.broadcasted_iota(jnp.int32, (Q_TILE, K_TILE), 0)
    col = lax.broadcasted_iota(jnp.int32, (Q_TILE, K_TILE), 1)
    return row, col


def _fox_prompt_kernel(q_ref, k_ref, v_ref, cum_ref, o_ref):
    qi = pl.program_id(2)
    row, col = _tile_iota()
    causal = col <= row
    outs = []
    for j in range(HEADS_PER_STEP):
        hs = slice(j * HEAD_DIM, (j + 1) * HEAD_DIM)
        q = q_ref[:, hs]

        def scores(kb):
            ks = pl.multiple_of(kb * K_TILE, K_TILE)
            s = _dot_nt(q, k_ref[pl.ds(ks, K_TILE), hs])
            return s - cum_ref[0, 0, j:j + 1, pl.ds(ks, K_TILE)], ks

        def update(carry, s, ks):
            m, l, acc = carry
            m_new = jnp.maximum(m, jnp.max(s, axis=-1, keepdims=True))
            p = jnp.exp(s - m_new)
            alpha = jnp.exp(m - m_new)
            l = alpha * l + jnp.sum(p, axis=-1, keepdims=True)
            acc = alpha * acc + _dot(p.astype(BF16), v_ref[pl.ds(ks, K_TILE), hs])
            return m_new, l, acc

        def body(kb, carry):
            s, ks = scores(kb)
            return update(carry, s, ks)

        init = (jnp.full((Q_TILE, 1), NEG_INF, F32), jnp.zeros((Q_TILE, 1), F32),
                jnp.zeros((Q_TILE, HEAD_DIM), F32))
        carry = lax.fori_loop(0, qi, body, init)
        s, ks = scores(qi)
        m, l, acc = update(carry, jnp.where(causal, s, NEG_INF), ks)
        outs.append(acc / l)
    o_ref[...] = jnp.concatenate(outs, axis=-1).astype(BF16)


def _fox_prompt(q, k, v, cum_t, b, s):
    q_spec, kv_spec = _prompt_specs(s)
    cum_spec = pl.BlockSpec((1, 1, HEADS_PER_STEP, s), lambda bi, hp, qi: (bi, hp, 0, 0))
    return pl.pallas_call(
        _fox_prompt_kernel,
        grid=(b, N_HEADS // HEADS_PER_STEP, s // Q_TILE),
        in_specs=[q_spec, kv_spec, kv_spec, cum_spec],
        out_specs=q_spec,
        out_shape=jax.ShapeDtypeStruct(q.shape, BF16),
        compiler_params=_params("parallel", "parallel", "arbitrary"),
        name="fox_prompt",
    )(q, k, v, cum_t)


def _stick_prompt_kernel(q_ref, k_ref, v_ref, o_ref):
    qi = pl.program_id(2)
    row, col = _tile_iota()
    earlier = col < row
    suffix = jnp.where(row > col, 1.0, 0.0).astype(BF16)
    outs = []
    for j in range(HEADS_PER_STEP):
        hs = slice(j * HEAD_DIM, (j + 1) * HEAD_DIM)
        q = q_ref[:, hs]

        def tile(kb, carry, diagonal):
            later, acc = carry
            ks = pl.multiple_of(kb * K_TILE, K_TILE)
            z = _dot_nt(q, k_ref[pl.ds(ks, K_TILE), hs])
            log_beta = _log_sigmoid(z)
            log_keep = log_beta - z
            if diagonal:
                log_keep = jnp.where(earlier, log_keep, 0.0)
            hi, lo = _split_bf16(log_keep)
            tail = _dot(hi, suffix) + _dot(lo, suffix) + later
            w = jnp.exp(log_beta + tail)
            if diagonal:
                w = jnp.where(earlier, w, 0.0)
            acc = acc + _dot(w.astype(BF16), v_ref[pl.ds(ks, K_TILE), hs])
            later = later + jnp.sum(log_keep, axis=-1, keepdims=True)
            return later, acc

        carry = (jnp.zeros((Q_TILE, 1), F32), jnp.zeros((Q_TILE, HEAD_DIM), F32))
        carry = tile(qi, carry, True)
        carry = lax.fori_loop(0, qi, lambda t, c: tile(qi - 1 - t, c, False), carry)
        outs.append(carry[1])
    o_ref[...] = jnp.concatenate(outs, axis=-1).astype(BF16)


def _stick_prompt(q, k, v, b, s):
    q_spec, kv_spec = _prompt_specs(s)
    return pl.pallas_call(
        _stick_prompt_kernel,
        grid=(b, N_HEADS // HEADS_PER_STEP, s // Q_TILE),
        in_specs=[q_spec, kv_spec, kv_spec],
        out_specs=q_spec,
        out_shape=jax.ShapeDtypeStruct(q.shape, BF16),
        compiler_params=_params("parallel", "parallel", "arbitrary"),
        name="stick_prompt",
    )(q, k, v)


BAND_TILE = Q_TILE + A_WINDOW
BIAS_WIDTH = 1024


def _band_prompt_kernel(q_ref, k_ref, v_ref, base_ref, o_ref, bias_ref):
    b, qi = pl.program_id(1), pl.program_id(2)

    @pl.when((b == 0) & (qi == 0))
    def _():
        i = lax.broadcasted_iota(jnp.int32, (Q_TILE, BAND_TILE), 0) // CHUNK
        m = lax.broadcasted_iota(jnp.int32, (Q_TILE, BAND_TILE), 1) // CHUNK
        in_band = (m >= i) & (m <= i + LEFT_CHUNKS)
        for j in range(HEADS_PER_STEP):
            rows = jnp.broadcast_to(base_ref[0, j:j + 1, :], (Q_TILE, BIAS_WIDTH))
            rolled = pltpu.roll(rows, 0, 1, stride=1, stride_axis=0)
            bias_ref[j] = jnp.where(in_band, rolled[:, :BAND_TILE], NEG_INF)

    def attend(key_start, width):
        outs = []
        for j in range(HEADS_PER_STEP):
            hs = slice(j * HEAD_DIM, (j + 1) * HEAD_DIM)
            s = _dot_nt(q_ref[:, hs], k_ref[pl.ds(key_start, width), hs])
            s = s + bias_ref[j, :, BAND_TILE - width:]
            p = jnp.exp(s - jnp.max(s, axis=-1, keepdims=True))
            p = p / jnp.sum(p, axis=-1, keepdims=True)
            outs.append(_dot(p.astype(BF16), v_ref[pl.ds(key_start, width), hs]))
        o_ref[...] = jnp.concatenate(outs, axis=-1).astype(BF16)

    n_lead = A_WINDOW // Q_TILE
    for t in range(n_lead):
        pl.when(qi == t)(functools.partial(attend, 0, (t + 1) * Q_TILE))

    @pl.when(qi >= n_lead)
    def _():
        attend(pl.multiple_of((qi - n_lead) * Q_TILE, Q_TILE), BAND_TILE)


def _band_prompt(q, k, v, base, b, s):
    q_spec, kv_spec = _prompt_specs(s, b_first=False)
    base_spec = pl.BlockSpec((1, HEADS_PER_STEP, BIAS_WIDTH), lambda hp, bi, qi: (hp, 0, 0))
    return pl.pallas_call(
        _band_prompt_kernel,
        grid=(N_HEADS // HEADS_PER_STEP, b, s // Q_TILE),
        in_specs=[q_spec, kv_spec, kv_spec, base_spec],
        out_specs=q_spec,
        out_shape=jax.ShapeDtypeStruct(q.shape, BF16),
        scratch_shapes=[pltpu.VMEM((HEADS_PER_STEP, Q_TILE, BAND_TILE), F32)],
        compiler_params=_params("arbitrary", "arbitrary", "arbitrary"),
        name="band_prompt",
    )(q, k, v, base)


def _rel_rows_desc(rel_table, hi, lo):
    top = jnp.broadcast_to(rel_table[-1], (max(0, hi - max(lo, REL_MAX + 1) + 1), rel_table.shape[1]))
    mid = rel_table[max(lo, -REL_MAX) + REL_MAX:min(hi, REL_MAX) + REL_MAX + 1][::-1]
    bot = jnp.broadcast_to(rel_table[0], (max(0, min(hi, -REL_MAX - 1) - lo + 1), rel_table.shape[1]))
    return jnp.concatenate([top, mid, bot], axis=0)


def _band_base(rel_table, q_rows, n_keys, width):
    offset = n_keys - q_rows
    desc = _rel_rows_desc(rel_table, offset + q_rows - 1, offset - n_keys + 1)
    pad = jnp.zeros((width - desc.shape[0], rel_table.shape[1]), rel_table.dtype)
    return jnp.concatenate([desc[q_rows - 1:], pad, desc[:q_rows - 1]], axis=0).T


def _sample_bias(rel_table, l, n_cache):
    n = n_cache + l
    desc = _rel_rows_desc(rel_table, n_cache + l - 1, -(l - 1))
    cols = jnp.stack([desc[l - 1 - i:l - 1 - i + n] for i in range(l)], axis=1)
    return jnp.swapaxes(cols, 1, 2).reshape(n, rel_table.shape[1] * l)


def _block_diag_q(q, l):
    d = q.shape[1]
    tiled = jnp.concatenate([q] * N_HEADS, axis=0)
    r = lax.broadcasted_iota(jnp.int32, (N_HEADS * l, d), 0) // l
    c = lax.broadcasted_iota(jnp.int32, (N_HEADS * l, d), 1) // HEAD_DIM
    return jnp.where(r == c, tiled, jnp.zeros_like(tiled))


def _gather_heads(r, l):
    d = r.shape[1]
    rr = lax.broadcasted_iota(jnp.int32, (N_HEADS * l, d), 0) // l
    c = lax.broadcasted_iota(jnp.int32, (N_HEADS * l, d), 1) // HEAD_DIM
    r = jnp.where(rr == c, r, 0.0)
    out = r[:l]
    for h in range(1, N_HEADS):
        out = out + r[h * l:(h + 1) * l]
    return out


def _new_key_masks(l, strict):
    j = lax.broadcasted_iota(jnp.int32, (l, N_HEADS * l), 0)
    i = lax.broadcasted_iota(jnp.int32, (l, N_HEADS * l), 1) % l
    return (j < i) if strict else (j <= i)


def _band_sample_kernel(q_ref, kn_ref, vn_ref, kc_ref, vc_ref, bias_ref, o_ref):
    l = q_ref.shape[0]
    qb = _block_diag_q(q_ref[...], l)
    kk = jnp.concatenate([kc_ref[0].astype(BF16), kn_ref[...]], axis=0)
    vv = jnp.concatenate([vc_ref[0].astype(BF16), vn_ref[...]], axis=0)
    s = _dot_nt(kk, qb) + bias_ref[...]
    p = jnp.exp(s - jnp.max(s, axis=0, keepdims=True))
    p = p / jnp.sum(p, axis=0, keepdims=True)
    o_ref[...] = _gather_heads(_dot_tn(p.astype(BF16), vv), l).astype(BF16)


def _fox_sample_kernel(q_ref, kn_ref, vn_ref, kc_ref, vc_ref, fc_ref, fn_ref, o_ref):
    l = q_ref.shape[0]
    past = kc_ref.shape[1]
    n = past + l
    qb = _block_diag_q(q_ref[...], l)
    kk = jnp.concatenate([kc_ref[0].astype(BF16), kn_ref[...]], axis=0)
    vv = jnp.concatenate([vc_ref[0].astype(BF16), vn_ref[...]], axis=0)
    log_f = jnp.concatenate([fc_ref[0], fn_ref[:, :N_HEADS]], axis=0)
    eh = lax.broadcasted_iota(jnp.int32, (N_HEADS, N_HEADS * l), 0)
    ec = lax.broadcasted_iota(jnp.int32, (N_HEADS, N_HEADS * l), 1) // l
    expand = jnp.where(eh == ec, 1.0, 0.0).astype(BF16)
    upto = jnp.where(lax.broadcasted_iota(jnp.int32, (n, n), 1)
                     <= lax.broadcasted_iota(jnp.int32, (n, n), 0), 1.0, 0.0).astype(BF16)
    f_hi, f_lo = _split_bf16(log_f)
    f_lo2 = (log_f - f_hi.astype(F32) - f_lo.astype(F32)).astype(BF16)
    spread = _dot(f_hi, expand) + _dot(f_lo, expand) + _dot(f_lo2, expand)
    s_hi, s_lo = _split_bf16(spread)
    s_lo2 = (spread - s_hi.astype(F32) - s_lo.astype(F32)).astype(BF16)
    cum = _dot(upto, s_hi) + _dot(upto, s_lo) + _dot(upto, s_lo2)
    s = _dot_nt(kk, qb) - cum
    visible = _new_key_masks(l, strict=False)
    s = jnp.concatenate([s[:past], jnp.where(visible, s[past:], NEG_INF)], axis=0)
    p = jnp.exp(s - jnp.max(s, axis=0, keepdims=True))
    p = p / jnp.sum(p, axis=0, keepdims=True)
    o_ref[...] = _gather_heads(_dot_tn(p.astype(BF16), vv), l).astype(BF16)


def _stick_sample_kernel(q_ref, kn_ref, vn_ref, kc_ref, vc_ref, o_ref):
    l = q_ref.shape[0]
    past = kc_ref.shape[1]
    n = past + l
    qb = _block_diag_q(q_ref[...], l)
    kk = jnp.concatenate([kc_ref[0].astype(BF16), kn_ref[...]], axis=0)
    vv = jnp.concatenate([vc_ref[0].astype(BF16), vn_ref[...]], axis=0)
    z = _dot_nt(kk, qb)
    log_beta = _log_sigmoid(z)
    log_keep = log_beta - z
    earlier = _new_key_masks(l, strict=True)
    log_keep = jnp.concatenate([log_keep[:past], jnp.where(earlier, log_keep[past:], 0.0)], axis=0)
    after = jnp.where(lax.broadcasted_iota(jnp.int32, (n, n), 1)
                      > lax.broadcasted_iota(jnp.int32, (n, n), 0), 1.0, 0.0).astype(BF16)
    hi, lo = _split_bf16(log_keep)
    tail = _dot(after, hi) + _dot(after, lo)
    w = jnp.exp(log_beta + tail)
    w = jnp.concatenate([w[:past], jnp.where(earlier, w[past:], 0.0)], axis=0)
    o_ref[...] = _gather_heads(_dot_tn(w.astype(BF16), vv), l).astype(BF16)


def _sample_call(kernel, name, q, k_new, v_new, cache_k, cache_v, extra=(), extra_specs=()):
    nb, past, d = cache_k.shape
    l = q.shape[0] // nb
    new = pl.BlockSpec((l, d), lambda b: (b, 0))
    cache = pl.BlockSpec((1, past, d), lambda b: (b, 0, 0))
    return pl.pallas_call(
        kernel,
        grid=(nb,),
        in_specs=[new, new, new, cache, cache, *extra_specs],
        out_specs=new,
        out_shape=jax.ShapeDtypeStruct(q.shape, BF16),
        compiler_params=_params("parallel"),
        name=name,
    )(q, k_new, v_new, cache_k, cache_v, *extra)


def _cumsum_kernel(f_ref, o_ref):
    s = f_ref.shape[2]
    upto = jnp.where(lax.broadcasted_iota(jnp.int32, (K_TILE, K_TILE), 0)
                     <= lax.broadcasted_iota(jnp.int32, (K_TILE, K_TILE), 1), 1.0, 0.0).astype(BF16)
    carry = jnp.zeros((N_HEADS, 1), F32)
    for t in range(s // K_TILE):
        x = f_ref[0, :, t * K_TILE:(t + 1) * K_TILE]
        hi, lo = _split_bf16(x)
        lo2 = (x - hi.astype(F32) - lo.astype(F32)).astype(BF16)
        c = _dot(hi, upto) + _dot(lo, upto) + _dot(lo2, upto) + carry
        o_ref[0, :, t * K_TILE:(t + 1) * K_TILE] = c
        carry = c[:, K_TILE - 1:]


def _cumsum_t(log_f_t):
    b, h, s = log_f_t.shape
    spec = pl.BlockSpec((1, h, s), lambda i: (i, 0, 0))
    return pl.pallas_call(
        _cumsum_kernel, grid=(b,), in_specs=[spec], out_specs=spec,
        out_shape=jax.ShapeDtypeStruct(log_f_t.shape, F32),
        compiler_params=_params("parallel"), name="forget_cumsum",
    )(log_f_t)


def kernel(x_prompt, x_sample, cache_a_k, cache_a_v, cache_b_k, cache_b_v, cache_b_logf, cache_c_k, cache_c_v, norm_ffn1, ffn1_gate, ffn1_up, ffn1_down, norm_mix, norm_ffn2, ffn2_gate, ffn2_up, ffn2_down, a_w_qkv, a_w_o, a_rel_bias, b_w_qkv, b_w_o, b_w_f, b_b_f, c_w_qkv, c_w_o, norm_final):
    b, s, d = x_prompt.shape
    nb, l, _ = x_sample.shape
    depth = norm_ffn1.shape[0]
    assert d == N_HEADS * HEAD_DIM and s % Q_TILE == 0 and s >= A_WINDOW

    xp = x_prompt.reshape(b * s, d)
    xs = x_sample.reshape(nb * l, d)
    heads = (N_HEADS, HEAD_DIM)
    outs = {name: [] for name in ("a_kp", "a_vp", "a_ks", "a_vs", "b_kp", "b_vp", "b_fp", "b_ks", "b_vs",
                                  "b_fs", "c_kp", "c_vp", "c_ks", "c_vs")}
    proj_p = proj_s = None
    for i in range(depth):
        kind, slot = i % 3, i // 3
        w1 = (ffn1_gate[i].astype(BF16), ffn1_up[i].astype(BF16), ffn1_down[i].astype(BF16))
        w2 = (ffn2_gate[i].astype(BF16), ffn2_up[i].astype(BF16), ffn2_down[i].astype(BF16))
        xp = _ffn(xp, norm_ffn1[i], *w1)
        xs = _ffn(xs, norm_ffn1[i], *w1)
        if kind == 0:
            w_qkv, w_o = a_w_qkv[slot].astype(BF16), a_w_o[slot].astype(BF16)
            qp, kp, vp, kpb, vpb = _qkv(xp, norm_mix[i], w_qkv)
            qs, ks, vs, ksb, vsb = _qkv(xs, norm_mix[i], w_qkv)
            base = _band_base(a_rel_bias[slot], Q_TILE, BAND_TILE, BIAS_WIDTH)
            base = base.reshape(N_HEADS // HEADS_PER_STEP, HEADS_PER_STEP, BIAS_WIDTH)
            op = _band_prompt(qp, kpb, vpb, base, b, s)
            n_cache = cache_a_k.shape[2]
            bias_s = _sample_bias(a_rel_bias[slot], l, n_cache)
            os_ = _sample_call(_band_sample_kernel, "band_sample", qs, ksb, vsb,
                               cache_a_k[slot].reshape(nb, n_cache, d), cache_a_v[slot].reshape(nb, n_cache, d),
                               extra=(bias_s,), extra_specs=(_resident(bias_s.shape),))
            keep = min(A_WINDOW, s)
            outs["a_kp"].append(kp.reshape(b, s, *heads)[:, s - keep:])
            outs["a_vp"].append(vp.reshape(b, s, *heads)[:, s - keep:])
            outs["a_ks"].append(ks.reshape(nb, l, *heads))
            outs["a_vs"].append(vs.reshape(nb, l, *heads))
        elif kind == 1:
            w_qkv, w_o = b_w_qkv[slot].astype(BF16), b_w_o[slot].astype(BF16)
            forget = (jnp.pad(b_w_f[slot], ((0, 0), (0, LANE - N_HEADS))).astype(BF16),
                      jnp.pad(b_b_f[slot], (0, LANE - N_HEADS)).reshape(1, LANE))
            qp, kp, vp, kpb, vpb, fp = _qkv(xp, norm_mix[i], w_qkv, forget)
            qs, ks, vs, ksb, vsb, fs = _qkv(xs, norm_mix[i], w_qkv, forget)
            fp = fp[:, :N_HEADS].reshape(b, s, N_HEADS)
            cum_t = _cumsum_t(jnp.swapaxes(fp, 1, 2))
            cum_t = cum_t.reshape(b, N_HEADS // HEADS_PER_STEP, HEADS_PER_STEP, s)
            op = _fox_prompt(qp, kpb, vpb, cum_t, b, s)
            past = cache_b_k.shape[2]
            os_ = _sample_call(_fox_sample_kernel, "fox_sample", qs, ksb, vsb,
                               cache_b_k[slot].reshape(nb, past, d), cache_b_v[slot].reshape(nb, past, d),
                               extra=(cache_b_logf[slot], fs),
                               extra_specs=(pl.BlockSpec((1, past, N_HEADS), lambda bi: (bi, 0, 0)),
                                            pl.BlockSpec((l, LANE), lambda bi: (bi, 0))))
            outs["b_kp"].append(kp.reshape(b, s, *heads))
            outs["b_vp"].append(vp.reshape(b, s, *heads))
            outs["b_fp"].append(fp)
            outs["b_ks"].append(ks.reshape(nb, l, *heads))
            outs["b_vs"].append(vs.reshape(nb, l, *heads))
            outs["b_fs"].append(fs[:, :N_HEADS].reshape(nb, l, N_HEADS))
        else:
            w_qkv, w_o = c_w_qkv[slot].astype(BF16), c_w_o[slot].astype(BF16)
            qp, kp, vp, kpb, vpb = _qkv(xp, norm_mix[i], w_qkv)
            qs, ks, vs, ksb, vsb = _qkv(xs, norm_mix[i], w_qkv)
            op = _stick_prompt(qp, kpb, vpb, b, s)
            past = cache_c_k.shape[2]
            os_ = _sample_call(_stick_sample_kernel, "stick_sample", qs, ksb, vsb,
                               cache_c_k[slot].reshape(nb, past, d), cache_c_v[slot].reshape(nb, past, d))
            outs["c_kp"].append(kp.reshape(b, s, *heads))
            outs["c_vp"].append(vp.reshape(b, s, *heads))
            outs["c_ks"].append(ks.reshape(nb, l, *heads))
            outs["c_vs"].append(vs.reshape(nb, l, *heads))
        final = norm_final if i == depth - 1 else None
        xp = _ffn(xp, norm_ffn2[i], *w2, proj=(op, w_o), final_g=final)
        xs = _ffn(xs, norm_ffn2[i], *w2, proj=(os_, w_o), final_g=final)

    order = ("a_kp", "a_vp", "a_ks", "a_vs", "b_kp", "b_vp", "b_fp", "b_ks", "b_vs", "b_fs",
             "c_kp", "c_vp", "c_ks", "c_vs")
    return (xp.reshape(b, s, d), xs.reshape(nb, l, d)) + tuple(jnp.stack(outs[name]) for name in order)
```

```python
import functools

import jax
import jax.numpy as jnp
from jax import lax
from jax.experimental import pallas as pl
from jax.experimental.pallas import tpu as pltpu

F32 = jnp.float32
BF16 = jnp.bfloat16

N_HEADS = 16
HEAD_DIM = 64
CHUNK = 64
LEFT_CHUNKS = 8
A_WINDOW = LEFT_CHUNKS * CHUNK
REL_MAX = 256
RMS_EPS = 1e-6
ATTN_SCALE = HEAD_DIM ** -0.5
NEG_INF = -1e30

LANE = 128
HEADS_PER_STEP = LANE // HEAD_DIM
N_HEAD_PAIRS = N_HEADS // HEADS_PER_STEP
VMEM_LIMIT = 56 * 1024 * 1024

ROW_TILE = 512
FF_TILE = 256
KV_TILE = 256
CAUSAL_TILE = 512
BAND_Q_TILE = 256
BAND_KEYS = BAND_Q_TILE + A_WINDOW
BIAS_WIDTH = 1024


def _params(*sem):
    return pltpu.CompilerParams(dimension_semantics=sem, vmem_limit_bytes=VMEM_LIMIT)


def _rms(x, g):
    y = x * lax.rsqrt(jnp.mean(x * x, axis=-1, keepdims=True) + RMS_EPS)
    return y * g


def _log_sigmoid(z):
    return jnp.minimum(z, 0.0) - jnp.log(1.0 + jnp.exp(-jnp.abs(z)))


def _dot(a, b):
    return jnp.dot(a, b, preferred_element_type=F32)


def _dot_nt(a, b):
    return lax.dot_general(a, b, (((1,), (1,)), ((), ())), preferred_element_type=F32)


def _dot_tn(a, b):
    return lax.dot_general(a, b, (((0,), (0,)), ((), ())), preferred_element_type=F32)


def _split2(x):
    hi = x.astype(BF16)
    lo = (x - hi.astype(F32)).astype(BF16)
    return hi, lo


def _split3(x):
    hi, lo = _split2(x)
    lo2 = (x - hi.astype(F32) - lo.astype(F32)).astype(BF16)
    return hi, lo, lo2


def _resident(shape):
    return pl.BlockSpec(shape, lambda *_: (0,) * len(shape), pipeline_mode=pl.Buffered(1))


def _ffn_kernel(*refs, has_proj, has_final):
    refs = list(refs)
    x_ref = refs.pop(0)
    if has_proj:
        o_ref, wo_ref = refs.pop(0), refs.pop(0)
    g_ref, wg_ref, wu_ref, wd_ref = refs[:4]
    refs = refs[4:]
    if has_final:
        gf_ref = refs.pop(0)
    out_ref = refs.pop(0)

    x = x_ref[...]
    if has_proj:
        x = x + _dot(o_ref[...], wo_ref[...])
    h = _rms(x, g_ref[...]).astype(BF16)
    d_ff = wg_ref.shape[1]
    acc = jnp.zeros(x.shape, F32)
    for c in range(d_ff // FF_TILE):
        sl = slice(c * FF_TILE, (c + 1) * FF_TILE)
        gate = _dot(h, wg_ref[:, sl])
        up = _dot(h, wu_ref[:, sl])
        act = (gate * jax.nn.sigmoid(gate) * up).astype(BF16)
        acc = acc + _dot(act, wd_ref[sl, :])
    y = x + 0.5 * acc
    if has_final:
        y = _rms(y, gf_ref[...])
    out_ref[...] = y


def _ffn(x, g, wg, wu, wd, proj=None, final_g=None):
    n, d = x.shape
    d_ff = wg.shape[1]
    tm = min(ROW_TILE, n)
    assert n % tm == 0 and d_ff % FF_TILE == 0
    row = pl.BlockSpec((tm, d), lambda i: (i, 0))
    args, specs = [x], [row]
    if proj is not None:
        args += list(proj)
        specs += [row, _resident((d, d))]
    args += [g.reshape(1, d), wg, wu, wd]
    specs += [_resident((1, d)), _resident((d, d_ff)), _resident((d, d_ff)), _resident((d_ff, d))]
    if final_g is not None:
        args.append(final_g.reshape(1, d))
        specs.append(_resident((1, d)))
    return pl.pallas_call(
        functools.partial(_ffn_kernel, has_proj=proj is not None, has_final=final_g is not None),
        grid=(n // tm,),
        in_specs=specs,
        out_specs=row,
        out_shape=jax.ShapeDtypeStruct((n, d), F32),
        compiler_params=_params("parallel"),
        name="ffn",
    )(*args)


def _qkv_kernel(*refs, has_forget, transposed_v):
    x_ref, g_ref, w_ref, wv_ref = refs[:4]
    refs = refs[4:]
    if has_forget:
        wf_ref, bf_ref = refs[:2]
        refs = refs[2:]
    q_ref, k_ref, v_ref, kb_ref, vb_ref = refs[:5]
    d = x_ref.shape[1]
    h = _rms(x_ref[...], g_ref[...]).astype(BF16)
    q_ref[...] = (_dot(h, w_ref[:, :d]) * ATTN_SCALE).astype(BF16)
    k = _dot(h, w_ref[:, d:])
    k_ref[...] = k
    kb_ref[...] = k.astype(BF16)
    if transposed_v:
        v_ref[...] = _dot_nt(h, wv_ref[...])
        vt = _dot_nt(wv_ref[...], h).astype(BF16)
        for t in range(vb_ref.shape[0]):
            vb_ref[t] = vt[:, t * KV_TILE:(t + 1) * KV_TILE]
    else:
        v = _dot(h, wv_ref[...])
        v_ref[...] = v
        vb_ref[...] = v.astype(BF16)
    if has_forget:
        refs[5][...] = _log_sigmoid(_dot(h, wf_ref[...]) + bf_ref[...])


def _qkv(x, g, w_qk, w_v, forget=None, transposed_v=False):
    n, d = x.shape
    tm = min(ROW_TILE, n)
    assert n % tm == 0 and tm % KV_TILE == 0
    row = pl.BlockSpec((tm, d), lambda i: (i, 0))
    args = [x, g.reshape(1, d), w_qk, w_v]
    specs = [row, _resident((1, d)), _resident((d, 2 * d)), _resident((d, d))]
    if transposed_v:
        vb_shape = jax.ShapeDtypeStruct((n // KV_TILE, d, KV_TILE), BF16)
        vb_spec = pl.BlockSpec((tm // KV_TILE, d, KV_TILE), lambda i: (i, 0, 0))
    else:
        vb_shape, vb_spec = jax.ShapeDtypeStruct((n, d), BF16), row
    shapes = [jax.ShapeDtypeStruct((n, d), BF16), jax.ShapeDtypeStruct((n, d), F32),
              jax.ShapeDtypeStruct((n, d), F32), jax.ShapeDtypeStruct((n, d), BF16), vb_shape]
    out_specs = [row, row, row, row, vb_spec]
    if forget is not None:
        args += list(forget)
        specs += [_resident((d, LANE)), _resident((1, LANE))]
        shapes.append(jax.ShapeDtypeStruct((n, LANE), F32))
        out_specs.append(pl.BlockSpec((tm, LANE), lambda i: (i, 0)))
    return pl.pallas_call(
        functools.partial(_qkv_kernel, has_forget=forget is not None, transposed_v=transposed_v),
        grid=(n // tm,),
        in_specs=specs,
        out_specs=out_specs,
        out_shape=shapes,
        compiler_params=_params("parallel"),
        name="qkv",
    )(*args)


def _pair_operands(q_ref):
    q = q_ref[...]
    lane_head = lax.broadcasted_iota(jnp.int32, q.shape, 1) // HEAD_DIM
    return [jnp.where(lane_head == j, q, jnp.zeros_like(q)) for j in range(HEADS_PER_STEP)]


def _vt_tile(vt_ref, t, j, with_ones):
    vt = vt_ref[t]
    if not with_ones:
        return vt
    row_head = lax.broadcasted_iota(jnp.int32, vt.shape, 0) // HEAD_DIM
    return jnp.where(row_head == j, vt, jnp.ones_like(vt))


def _pv_t(vt_ref, first_tile, p, j, with_ones):
    out = None
    for t in range(p.shape[0] // KV_TILE):
        part = _dot(_vt_tile(vt_ref, first_tile + t, j, with_ones), p[t * KV_TILE:(t + 1) * KV_TILE])
        out = part if out is None else out + part
    return out


def _merge_heads_t(per_head, normalise):
    parts = []
    for j, acc in enumerate(per_head):
        own = acc[j * HEAD_DIM:(j + 1) * HEAD_DIM]
        if normalise:
            other = (1 - j) * HEAD_DIM
            own = own / acc[other:other + 1]
        parts.append(own)
    return jnp.concatenate(parts, axis=0).T.astype(BF16)


def _prompt_specs(s, tq, batch_first=True):
    nq = s // tq
    if batch_first:
        q_map = lambda b, hp, qi: (b * nq + qi, hp)
        k_map = lambda b, hp, qi: (b, hp)
        vt_map = lambda b, hp, qi: (b, hp, 0)
    else:
        q_map = lambda hp, b, qi: (b * nq + qi, hp)
        k_map = lambda hp, b, qi: (b, hp)
        vt_map = lambda hp, b, qi: (b, hp, 0)
    return (pl.BlockSpec((tq, LANE), q_map), pl.BlockSpec((s, LANE), k_map),
            pl.BlockSpec((s // KV_TILE, LANE, KV_TILE), vt_map))


def _fox_prompt_kernel(q_ref, k_ref, vt_ref, cum_ref, o_ref, cumrep_ref, m_ref, acc_ref):
    hp, qi = pl.program_id(1), pl.program_id(2)
    tq = q_ref.shape[0]
    reps = tq // LANE

    @pl.when(qi == 0)
    def _():
        c = cum_ref[...]
        lane = lax.broadcasted_iota(jnp.int32, c.shape, 1)
        for j in range(HEADS_PER_STEP):
            col = jnp.sum(jnp.where(lane == hp * HEADS_PER_STEP + j, c, 0.0), axis=1, keepdims=True)
            cumrep_ref[j] = jnp.broadcast_to(col, c.shape)

    q_pad = _pair_operands(q_ref)
    m_ref[...] = jnp.full(m_ref.shape, NEG_INF, F32)
    acc_ref[...] = jnp.zeros(acc_ref.shape, F32)
    visible = (lax.broadcasted_iota(jnp.int32, (tq, tq), 0) <= lax.broadcasted_iota(jnp.int32, (tq, tq), 1))

    def step(kb, diagonal):
        ks = pl.multiple_of(kb * tq, tq)
        k = k_ref[pl.ds(ks, tq), :]
        for j in range(HEADS_PER_STEP):
            s = _dot_nt(k, q_pad[j])
            s = s - jnp.concatenate([cumrep_ref[j, pl.ds(ks, tq), :]] * reps, axis=1)
            if diagonal:
                s = jnp.where(visible, s, NEG_INF)
            m_old = m_ref[j]
            m_new = jnp.maximum(m_old, jnp.max(s, axis=0, keepdims=True))
            p = jnp.exp(s - m_new).astype(BF16)
            acc_ref[j] = jnp.exp(m_old - m_new) * acc_ref[j] + _pv_t(vt_ref, kb * (tq // KV_TILE), p, j, True)
            m_ref[j] = m_new

    def body(kb, carry):
        step(kb, False)
        return carry

    lax.fori_loop(0, qi, body, 0)
    step(qi, True)
    o_ref[...] = _merge_heads_t([acc_ref[j] for j in range(HEADS_PER_STEP)], True)


def _fox_prompt(q, k, vt, cum, b, s):
    tq = CAUSAL_TILE
    q_spec, k_spec, vt_spec = _prompt_specs(s, tq)
    cum_spec = pl.BlockSpec((s, LANE), lambda bi, hp, qi: (bi, 0))
    return pl.pallas_call(
        _fox_prompt_kernel,
        grid=(b, N_HEAD_PAIRS, s // tq),
        in_specs=[q_spec, k_spec, vt_spec, cum_spec],
        out_specs=q_spec,
        out_shape=jax.ShapeDtypeStruct(q.shape, BF16),
        scratch_shapes=[pltpu.VMEM((HEADS_PER_STEP, s, LANE), F32),
                        pltpu.VMEM((HEADS_PER_STEP, 1, tq), F32),
                        pltpu.VMEM((HEADS_PER_STEP, LANE, tq), F32)],
        compiler_params=_params("parallel", "parallel", "arbitrary"),
        name="fox_prompt",
    )(q, k, vt, cum)


def _stick_prompt_kernel(q_ref, k_ref, vt_ref, o_ref, later_ref, acc_ref):
    qi = pl.program_id(2)
    tq = q_ref.shape[0]
    q_pad = _pair_operands(q_ref)
    later_ref[...] = jnp.zeros(later_ref.shape, F32)
    acc_ref[...] = jnp.zeros(acc_ref.shape, F32)
    earlier = lax.broadcasted_iota(jnp.int32, (tq, tq), 0) < lax.broadcasted_iota(jnp.int32, (tq, tq), 1)
    after = jnp.where(lax.broadcasted_iota(jnp.int32, (KV_TILE, KV_TILE), 1)
                      > lax.broadcasted_iota(jnp.int32, (KV_TILE, KV_TILE), 0), 1.0, 0.0).astype(BF16)
    after2 = jnp.concatenate([after, after], axis=1)

    def step(kb, diagonal):
        ks = pl.multiple_of(kb * tq, tq)
        k = k_ref[pl.ds(ks, tq), :]
        for j in range(HEADS_PER_STEP):
            z = _dot_nt(k, q_pad[j])
            log_beta = _log_sigmoid(z)
            log_keep = log_beta - z
            if diagonal:
                log_keep = jnp.where(earlier, log_keep, 0.0)
            later = later_ref[j]
            tails = []
            for t in reversed(range(tq // KV_TILE)):
                x = log_keep[t * KV_TILE:(t + 1) * KV_TILE]
                inner = _dot(after2, jnp.concatenate(_split2(x), axis=0))
                tails.append(inner + later)
                later = later + (inner[:1] + x[:1])
            later_ref[j] = later
            w = jnp.exp(log_beta + jnp.concatenate(tails[::-1], axis=0))
            if diagonal:
                w = jnp.where(earlier, w, 0.0)
            acc_ref[j] = acc_ref[j] + _pv_t(vt_ref, kb * (tq // KV_TILE), w.astype(BF16), j, False)

    step(qi, True)

    def body(t, carry):
        step(qi - 1 - t, False)
        return carry

    lax.fori_loop(0, qi, body, 0)
    o_ref[...] = _merge_heads_t([acc_ref[j] for j in range(HEADS_PER_STEP)], False)


def _stick_prompt(q, k, vt, b, s):
    tq = CAUSAL_TILE
    q_spec, k_spec, vt_spec = _prompt_specs(s, tq)
    return pl.pallas_call(
        _stick_prompt_kernel,
        grid=(b, N_HEAD_PAIRS, s // tq),
        in_specs=[q_spec, k_spec, vt_spec],
        out_specs=q_spec,
        out_shape=jax.ShapeDtypeStruct(q.shape, BF16),
        scratch_shapes=[pltpu.VMEM((HEADS_PER_STEP, 1, tq), F32),
                        pltpu.VMEM((HEADS_PER_STEP, LANE, tq), F32)],
        compiler_params=_params("parallel", "parallel", "arbitrary"),
        name="stick_prompt",
    )(q, k, vt)


def _band_prompt_kernel(q_ref, k_ref, vt_ref, base_ref, o_ref, bias_ref):
    b, qi = pl.program_id(1), pl.program_id(2)
    tq = q_ref.shape[0]

    @pl.when((b == 0) & (qi == 0))
    def _():
        i = lax.broadcasted_iota(jnp.int32, (tq, BAND_KEYS), 0) // CHUNK
        m = lax.broadcasted_iota(jnp.int32, (tq, BAND_KEYS), 1) // CHUNK
        in_band = (m >= i) & (m <= i + LEFT_CHUNKS)
        for j in range(HEADS_PER_STEP):
            rows = jnp.broadcast_to(base_ref[0, j:j + 1, :], (tq, BIAS_WIDTH))
            rolled = pltpu.roll(rows, 0, 1, stride=1, stride_axis=0)
            bias_ref[j] = jnp.where(in_band, rolled[:, :BAND_KEYS], NEG_INF).T

    q_pad = _pair_operands(q_ref)

    def attend(first_tile, n_tiles):
        width = n_tiles * KV_TILE
        start = first_tile * KV_TILE
        if not isinstance(start, int):
            start = pl.multiple_of(start, KV_TILE)
        k = k_ref[pl.ds(start, width), :]
        per_head = []
        for j in range(HEADS_PER_STEP):
            s = _dot_nt(k, q_pad[j]) + bias_ref[j, BAND_KEYS - width:, :]
            p = jnp.exp(s - jnp.max(s, axis=0, keepdims=True)).astype(BF16)
            per_head.append(_pv_t(vt_ref, first_tile, p, j, True))
        o_ref[...] = _merge_heads_t(per_head, True)

    n_lead = A_WINDOW // tq
    for t in range(n_lead):
        pl.when(qi == t)(functools.partial(attend, 0, t + 1))

    @pl.when(qi >= n_lead)
    def _():
        attend(qi - n_lead, BAND_KEYS // KV_TILE)


def _band_prompt(q, k, vt, base, b, s):
    tq = BAND_Q_TILE
    q_spec, k_spec, vt_spec = _prompt_specs(s, tq, batch_first=False)
    base_spec = pl.BlockSpec((1, HEADS_PER_STEP, BIAS_WIDTH), lambda hp, bi, qi: (hp, 0, 0))
    return pl.pallas_call(
        _band_prompt_kernel,
        grid=(N_HEAD_PAIRS, b, s // tq),
        in_specs=[q_spec, k_spec, vt_spec, base_spec],
        out_specs=q_spec,
        out_shape=jax.ShapeDtypeStruct(q.shape, BF16),
        scratch_shapes=[pltpu.VMEM((HEADS_PER_STEP, BAND_KEYS, tq), F32)],
        compiler_params=_params("arbitrary", "arbitrary", "arbitrary"),
        name="band_prompt",
    )(q, k, vt, base)


def _rel_rows_desc(rel_table, hi, lo):
    top = jnp.broadcast_to(rel_table[-1], (max(0, hi - max(lo, REL_MAX + 1) + 1), rel_table.shape[1]))
    mid = rel_table[max(lo, -REL_MAX) + REL_MAX:min(hi, REL_MAX) + REL_MAX + 1][::-1]
    bot = jnp.broadcast_to(rel_table[0], (max(0, min(hi, -REL_MAX - 1) - lo + 1), rel_table.shape[1]))
    return jnp.concatenate([top, mid, bot], axis=0)


def _band_base(rel_table, q_rows, n_keys, width):
    offset = n_keys - q_rows
    desc = _rel_rows_desc(rel_table, offset + q_rows - 1, offset - n_keys + 1)
    pad = jnp.zeros((width - desc.shape[0], rel_table.shape[1]), rel_table.dtype)
    return jnp.concatenate([desc[q_rows - 1:], pad, desc[:q_rows - 1]], axis=0).T


def _sample_bias(rel_table, l, n_cache):
    n = n_cache + l
    desc = _rel_rows_desc(rel_table, n_cache + l - 1, -(l - 1))
    cols = jnp.stack([desc[l - 1 - i:l - 1 - i + n] for i in range(l)], axis=1)
    return jnp.swapaxes(cols, 1, 2).reshape(n, rel_table.shape[1] * l)


def _block_diag_q(q, l):
    d = q.shape[1]
    tiled = jnp.concatenate([q] * N_HEADS, axis=0)
    r = lax.broadcasted_iota(jnp.int32, (N_HEADS * l, d), 0) // l
    c = lax.broadcasted_iota(jnp.int32, (N_HEADS * l, d), 1) // HEAD_DIM
    return jnp.where(r == c, tiled, jnp.zeros_like(tiled))


def _gather_heads(r, l):
    d = r.shape[1]
    rr = lax.broadcasted_iota(jnp.int32, (N_HEADS * l, d), 0) // l
    c = lax.broadcasted_iota(jnp.int32, (N_HEADS * l, d), 1) // HEAD_DIM
    r = jnp.where(rr == c, r, 0.0)
    out = r[:l]
    for h in range(1, N_HEADS):
        out = out + r[h * l:(h + 1) * l]
    return out


def _new_key_masks(l, strict):
    j = lax.broadcasted_iota(jnp.int32, (l, N_HEADS * l), 0)
    i = lax.broadcasted_iota(jnp.int32, (l, N_HEADS * l), 1) % l
    return (j < i) if strict else (j <= i)


def _band_sample_kernel(q_ref, kn_ref, vn_ref, kc_ref, vc_ref, bias_ref, o_ref):
    l = q_ref.shape[0]
    qb = _block_diag_q(q_ref[...], l)
    kk = jnp.concatenate([kc_ref[0].astype(BF16), kn_ref[...]], axis=0)
    vv = jnp.concatenate([vc_ref[0].astype(BF16), vn_ref[...]], axis=0)
    s = _dot_nt(kk, qb) + bias_ref[...]
    p = jnp.exp(s - jnp.max(s, axis=0, keepdims=True))
    p = p / jnp.sum(p, axis=0, keepdims=True)
    o_ref[...] = _gather_heads(_dot_tn(p.astype(BF16), vv), l).astype(BF16)


def _fox_sample_kernel(q_ref, kn_ref, vn_ref, kc_ref, vc_ref, fc_ref, fn_ref, o_ref):
    l = q_ref.shape[0]
    past = kc_ref.shape[1]
    n = past + l
    qb = _block_diag_q(q_ref[...], l)
    kk = jnp.concatenate([kc_ref[0].astype(BF16), kn_ref[...]], axis=0)
    vv = jnp.concatenate([vc_ref[0].astype(BF16), vn_ref[...]], axis=0)
    log_f = jnp.concatenate([fc_ref[0], fn_ref[:, :N_HEADS]], axis=0)
    eh = lax.broadcasted_iota(jnp.int32, (N_HEADS, N_HEADS * l), 0)
    ec = lax.broadcasted_iota(jnp.int32, (N_HEADS, N_HEADS * l), 1) // l
    expand = jnp.where(eh == ec, 1.0, 0.0).astype(BF16)
    upto = jnp.where(lax.broadcasted_iota(jnp.int32, (n, n), 1)
                     <= lax.broadcasted_iota(jnp.int32, (n, n), 0), 1.0, 0.0).astype(BF16)
    spread = sum(_dot(part, expand) for part in _split3(log_f))
    cum = sum(_dot(upto, part) for part in _split3(spread))
    s = _dot_nt(kk, qb) - cum
    visible = _new_key_masks(l, strict=False)
    s = jnp.concatenate([s[:past], jnp.where(visible, s[past:], NEG_INF)], axis=0)
    p = jnp.exp(s - jnp.max(s, axis=0, keepdims=True))
    p = p / jnp.sum(p, axis=0, keepdims=True)
    o_ref[...] = _gather_heads(_dot_tn(p.astype(BF16), vv), l).astype(BF16)


def _stick_sample_kernel(q_ref, kn_ref, vn_ref, kc_ref, vc_ref, o_ref):
    l = q_ref.shape[0]
    past = kc_ref.shape[1]
    n = past + l
    qb = _block_diag_q(q_ref[...], l)
    kk = jnp.concatenate([kc_ref[0].astype(BF16), kn_ref[...]], axis=0)
    vv = jnp.concatenate([vc_ref[0].astype(BF16), vn_ref[...]], axis=0)
    z = _dot_nt(kk, qb)
    log_beta = _log_sigmoid(z)
    log_keep = log_beta - z
    earlier = _new_key_masks(l, strict=True)
    log_keep = jnp.concatenate([log_keep[:past], jnp.where(earlier, log_keep[past:], 0.0)], axis=0)
    after = jnp.where(lax.broadcasted_iota(jnp.int32, (n, n), 1)
                      > lax.broadcasted_iota(jnp.int32, (n, n), 0), 1.0, 0.0).astype(BF16)
    hi, lo = _split2(log_keep)
    tail = _dot(after, hi) + _dot(after, lo)
    w = jnp.exp(log_beta + tail)
    w = jnp.concatenate([w[:past], jnp.where(earlier, w[past:], 0.0)], axis=0)
    o_ref[...] = _gather_heads(_dot_tn(w.astype(BF16), vv), l).astype(BF16)


def _sample_call(kernel, name, q, k_new, v_new, cache_k, cache_v, extra=(), extra_specs=()):
    nb, past, d = cache_k.shape
    l = q.shape[0] // nb
    new = pl.BlockSpec((l, d), lambda b: (b, 0))
    cache = pl.BlockSpec((1, past, d), lambda b: (b, 0, 0))
    return pl.pallas_call(
        kernel,
        grid=(nb,),
        in_specs=[new, new, new, cache, cache, *extra_specs],
        out_specs=new,
        out_shape=jax.ShapeDtypeStruct(q.shape, BF16),
        compiler_params=_params("parallel"),
        name=name,
    )(q, k_new, v_new, cache_k, cache_v, *extra)


def _cumsum_kernel(f_ref, o_ref):
    upto = jnp.where(lax.broadcasted_iota(jnp.int32, (KV_TILE, KV_TILE), 1)
                     <= lax.broadcasted_iota(jnp.int32, (KV_TILE, KV_TILE), 0), 1.0, 0.0).astype(BF16)
    carry = jnp.zeros((1, f_ref.shape[1]), F32)
    for t in range(f_ref.shape[0] // KV_TILE):
        rows = slice(t * KV_TILE, (t + 1) * KV_TILE)
        c = sum(_dot(upto, part) for part in _split3(f_ref[rows, :])) + carry
        o_ref[rows, :] = c
        carry = c[KV_TILE - 1:, :]


def _cumsum_rows(log_f, s):
    spec = pl.BlockSpec((s, log_f.shape[1]), lambda i: (i, 0))
    return pl.pallas_call(
        _cumsum_kernel, grid=(log_f.shape[0] // s,), in_specs=[spec], out_specs=spec,
        out_shape=jax.ShapeDtypeStruct(log_f.shape, F32),
        compiler_params=_params("parallel"), name="forget_cumsum",
    )(log_f)


def kernel(x_prompt, x_sample, cache_a_k, cache_a_v, cache_b_k, cache_b_v, cache_b_logf, cache_c_k, cache_c_v, norm_ffn1, ffn1_gate, ffn1_up, ffn1_down, norm_mix, norm_ffn2, ffn2_gate, ffn2_up, ffn2_down, a_w_qkv, a_w_o, a_rel_bias, b_w_qkv, b_w_o, b_w_f, b_b_f, c_w_qkv, c_w_o, norm_final):
    b, s, d = x_prompt.shape
    nb, l, _ = x_sample.shape
    depth = norm_ffn1.shape[0]
    assert d == N_HEADS * HEAD_DIM and s % CAUSAL_TILE == 0 and s >= A_WINDOW and A_WINDOW % BAND_Q_TILE == 0

    xp = x_prompt.reshape(b * s, d)
    xs = x_sample.reshape(nb * l, d)
    heads = (N_HEADS, HEAD_DIM)
    order = ("a_kp", "a_vp", "a_ks", "a_vs", "b_kp", "b_vp", "b_fp", "b_ks", "b_vs", "b_fs",
             "c_kp", "c_vp", "c_ks", "c_vs")
    outs = {name: [] for name in order}
    for i in range(depth):
        kind, slot = i % 3, i // 3
        w1 = (ffn1_gate[i].astype(BF16), ffn1_up[i].astype(BF16), ffn1_down[i].astype(BF16))
        w2 = (ffn2_gate[i].astype(BF16), ffn2_up[i].astype(BF16), ffn2_down[i].astype(BF16))
        xp = _ffn(xp, norm_ffn1[i], *w1)
        xs = _ffn(xs, norm_ffn1[i], *w1)
        w_qkv, w_o = ((a_w_qkv, a_w_o), (b_w_qkv, b_w_o), (c_w_qkv, c_w_o))[kind]
        w_qk = w_qkv[slot][:, :2 * d].astype(BF16)
        w_v = w_qkv[slot][:, 2 * d:].astype(BF16)
        w_o = w_o[slot].astype(BF16)
        forget = None
        if kind == 1:
            forget = (jnp.pad(b_w_f[slot], ((0, 0), (0, LANE - N_HEADS))).astype(BF16),
                      jnp.pad(b_b_f[slot], (0, LANE - N_HEADS)).reshape(1, LANE))
        qp, kp, vp, kpb, vpt, *fp = _qkv(xp, norm_mix[i], w_qk, w_v.T, forget, transposed_v=True)
        qs, ks, vs, ksb, vsb, *fs = _qkv(xs, norm_mix[i], w_qk, w_v, forget)
        if kind == 0:
            base = _band_base(a_rel_bias[slot], BAND_Q_TILE, BAND_KEYS, BIAS_WIDTH)
            base = base.reshape(N_HEAD_PAIRS, HEADS_PER_STEP, BIAS_WIDTH)
            op = _band_prompt(qp, kpb, vpt, base, b, s)
            n_cache = cache_a_k.shape[2]
            bias_s = _sample_bias(a_rel_bias[slot], l, n_cache)
            os_ = _sample_call(_band_sample_kernel, "band_sample", qs, ksb, vsb,
                               cache_a_k[slot].reshape(nb, n_cache, d), cache_a_v[slot].reshape(nb, n_cache, d),
                               extra=(bias_s,), extra_specs=(_resident(bias_s.shape),))
            keep = min(A_WINDOW, s)
            outs["a_kp"].append(kp.reshape(b, s, *heads)[:, s - keep:])
            outs["a_vp"].append(vp.reshape(b, s, *heads)[:, s - keep:])
            outs["a_ks"].append(ks.reshape(nb, l, *heads))
            outs["a_vs"].append(vs.reshape(nb, l, *heads))
        elif kind == 1:
            op = _fox_prompt(qp, kpb, vpt, _cumsum_rows(fp[0], s), b, s)
            past = cache_b_k.shape[2]
            os_ = _sample_call(_fox_sample_kernel, "fox_sample", qs, ksb, vsb,
                               cache_b_k[slot].reshape(nb, past, d), cache_b_v[slot].reshape(nb, past, d),
                               extra=(cache_b_logf[slot], fs[0]),
                               extra_specs=(pl.BlockSpec((1, past, N_HEADS), lambda bi: (bi, 0, 0)),
                                            pl.BlockSpec((l, LANE), lambda bi: (bi, 0))))
            outs["b_kp"].append(kp.reshape(b, s, *heads))
            outs["b_vp"].append(vp.reshape(b, s, *heads))
            outs["b_fp"].append(fp[0][:, :N_HEADS].reshape(b, s, N_HEADS))
            outs["b_ks"].append(ks.reshape(nb, l, *heads))
            outs["b_vs"].append(vs.reshape(nb, l, *heads))
            outs["b_fs"].append(fs[0][:, :N_HEADS].reshape(nb, l, N_HEADS))
        else:
            op = _stick_prompt(qp, kpb, vpt, b, s)
            past = cache_c_k.shape[2]
            os_ = _sample_call(_stick_sample_kernel, "stick_sample", qs, ksb, vsb,
                               cache_c_k[slot].reshape(nb, past, d), cache_c_v[slot].reshape(nb, past, d))
            outs["c_kp"].append(kp.reshape(b, s, *heads))
            outs["c_vp"].append(vp.reshape(b, s, *heads))
            outs["c_ks"].append(ks.reshape(nb, l, *heads))
            outs["c_vs"].append(vs.reshape(nb, l, *heads))
        final = norm_final if i == depth - 1 else None
        xp = _ffn(xp, norm_ffn2[i], *w2, proj=(op, w_o), final_g=final)
        xs = _ffn(xs, norm_ffn2[i], *w2, proj=(os_, w_o), final_g=final)

    return (xp.reshape(b, s, d), xs.reshape(nb, l, d)) + tuple(jnp.stack(outs[name]) for name in order)
```

```python
import functools

import jax
import jax.numpy as jnp
from jax import lax
from jax.experimental import pallas as pl
from jax.experimental.pallas import tpu as pltpu

F32 = jnp.float32
BF16 = jnp.bfloat16

N_HEADS = 16
HEAD_DIM = 64
CHUNK = 64
LEFT_CHUNKS = 8
A_WINDOW = LEFT_CHUNKS * CHUNK
REL_MAX = 256
RMS_EPS = 1e-6
ATTN_SCALE = HEAD_DIM ** -0.5
NEG_INF = -1e30

LANE = 128
HEADS_PER_STEP = LANE // HEAD_DIM
N_HEAD_PAIRS = N_HEADS // HEADS_PER_STEP
VMEM_LIMIT = 56 * 1024 * 1024

ROW_TILE = 512
FF_TILE = 256
KV_TILE = 256
CAUSAL_TILE = 512
BAND_Q_TILE = 256
BAND_TILES_PER_STEP = 4
BAND_KEYS = BAND_Q_TILE + A_WINDOW
BIAS_WIDTH = 1024


def _params(*sem):
    return pltpu.CompilerParams(dimension_semantics=sem, vmem_limit_bytes=VMEM_LIMIT)


def _rms(x, g):
    y = x * lax.rsqrt(jnp.mean(x * x, axis=-1, keepdims=True) + RMS_EPS)
    return y * g


def _log_sigmoid(z):
    return jnp.minimum(z, 0.0) - jnp.log(1.0 + jnp.exp(-jnp.abs(z)))


def _dot(a, b):
    return jnp.dot(a, b, preferred_element_type=F32)


def _dot_nt(a, b):
    return lax.dot_general(a, b, (((1,), (1,)), ((), ())), preferred_element_type=F32)


def _dot_tn(a, b):
    return lax.dot_general(a, b, (((0,), (0,)), ((), ())), preferred_element_type=F32)


def _split2(x):
    hi = x.astype(BF16)
    lo = (x - hi.astype(F32)).astype(BF16)
    return hi, lo


def _split3(x):
    hi, lo = _split2(x)
    lo2 = (x - hi.astype(F32) - lo.astype(F32)).astype(BF16)
    return hi, lo, lo2


def _resident(shape):
    return pl.BlockSpec(shape, lambda *_: (0,) * len(shape), pipeline_mode=pl.Buffered(1))


def _ffn_kernel(*refs, has_proj, has_final):
    refs = list(refs)
    x_ref = refs.pop(0)
    if has_proj:
        o_ref, wo_ref = refs.pop(0), refs.pop(0)
    g_ref, wg_ref, wu_ref, wd_ref = refs[:4]
    refs = refs[4:]
    if has_final:
        gf_ref = refs.pop(0)
    out_ref = refs.pop(0)

    x = x_ref[...]
    if has_proj:
        x = x + _dot(o_ref[...], wo_ref[...])
    h = _rms(x, g_ref[...]).astype(BF16)
    d_ff = wg_ref.shape[1]
    acc = jnp.zeros(x.shape, F32)
    for c in range(d_ff // FF_TILE):
        sl = slice(c * FF_TILE, (c + 1) * FF_TILE)
        gate = _dot(h, wg_ref[:, sl])
        up = _dot(h, wu_ref[:, sl])
        act = (gate * jax.nn.sigmoid(gate) * up).astype(BF16)
        acc = acc + _dot(act, wd_ref[sl, :])
    y = x + 0.5 * acc
    if has_final:
        y = _rms(y, gf_ref[...])
    out_ref[...] = y


def _ffn(x, g, wg, wu, wd, proj=None, final_g=None):
    n, d = x.shape
    d_ff = wg.shape[1]
    tm = min(ROW_TILE, n)
    assert n % tm == 0 and d_ff % FF_TILE == 0
    row = pl.BlockSpec((tm, d), lambda i: (i, 0))
    args, specs = [x], [row]
    if proj is not None:
        args += list(proj)
        specs += [row, _resident((d, d))]
    args += [g.reshape(1, d), wg, wu, wd]
    specs += [_resident((1, d)), _resident((d, d_ff)), _resident((d, d_ff)), _resident((d_ff, d))]
    if final_g is not None:
        args.append(final_g.reshape(1, d))
        specs.append(_resident((1, d)))
    return pl.pallas_call(
        functools.partial(_ffn_kernel, has_proj=proj is not None, has_final=final_g is not None),
        grid=(n // tm,),
        in_specs=specs,
        out_specs=row,
        out_shape=jax.ShapeDtypeStruct((n, d), F32),
        compiler_params=_params("parallel"),
        name="ffn",
    )(*args)


def _qkv_kernel(*refs, has_forget, transposed_v):
    x_ref, g_ref, w_ref = refs[:3]
    refs = refs[3:]
    if has_forget:
        wf_ref, bf_ref = refs[:2]
        refs = refs[2:]
    q_ref, k_ref, v_ref, kb_ref, vb_ref = refs[:5]
    d = x_ref.shape[1]
    h = _rms(x_ref[...], g_ref[...]).astype(BF16)
    q_ref[...] = (_dot(h, w_ref[:, :d]) * ATTN_SCALE).astype(BF16)
    k = _dot(h, w_ref[:, d:2 * d])
    k_ref[...] = k.reshape(k_ref.shape)
    kb_ref[...] = k.astype(BF16)
    v = _dot(h, w_ref[:, 2 * d:])
    v_ref[...] = v.reshape(v_ref.shape)
    if transposed_v:
        vt = v.T.astype(BF16)
        for t in range(vb_ref.shape[0]):
            vb_ref[t] = vt[:, t * KV_TILE:(t + 1) * KV_TILE]
    else:
        vb_ref[...] = v.astype(BF16)
    if has_forget:
        refs[5][...] = _log_sigmoid(_dot(h, wf_ref[...]) + bf_ref[...])


def _qkv(x, g, w, forget=None, transposed_v=False):
    n, d = x.shape
    tm = min(ROW_TILE, n)
    assert n % tm == 0 and tm % KV_TILE == 0
    row = pl.BlockSpec((tm, d), lambda i: (i, 0))
    cache_row = pl.BlockSpec((tm, N_HEADS, HEAD_DIM), lambda i: (i, 0, 0))
    args = [x, g.reshape(1, d), w]
    specs = [row, _resident((1, d)), _resident((d, 3 * d))]
    if transposed_v:
        vb_shape = jax.ShapeDtypeStruct((n // KV_TILE, d, KV_TILE), BF16)
        vb_spec = pl.BlockSpec((tm // KV_TILE, d, KV_TILE), lambda i: (i, 0, 0))
    else:
        vb_shape, vb_spec = jax.ShapeDtypeStruct((n, d), BF16), row
    shapes = [jax.ShapeDtypeStruct((n, d), BF16), jax.ShapeDtypeStruct((n, N_HEADS, HEAD_DIM), F32),
              jax.ShapeDtypeStruct((n, N_HEADS, HEAD_DIM), F32), jax.ShapeDtypeStruct((n, d), BF16), vb_shape]
    out_specs = [row, cache_row, cache_row, row, vb_spec]
    if forget is not None:
        args += list(forget)
        specs += [_resident((d, LANE)), _resident((1, LANE))]
        shapes.append(jax.ShapeDtypeStruct((n, LANE), F32))
        out_specs.append(pl.BlockSpec((tm, LANE), lambda i: (i, 0)))
    return pl.pallas_call(
        functools.partial(_qkv_kernel, has_forget=forget is not None, transposed_v=transposed_v),
        grid=(n // tm,),
        in_specs=specs,
        out_specs=out_specs,
        out_shape=shapes,
        compiler_params=_params("parallel"),
        name="qkv",
    )(*args)


def _pair_operands(q):
    lane_head = lax.broadcasted_iota(jnp.int32, q.shape, 1) // HEAD_DIM
    return [jnp.where(lane_head == j, q, jnp.zeros_like(q)) for j in range(HEADS_PER_STEP)]


def _vt_tile(vt_ref, t, j, with_ones):
    vt = vt_ref[t]
    if not with_ones:
        return vt
    row_head = lax.broadcasted_iota(jnp.int32, vt.shape, 0) // HEAD_DIM
    return jnp.where(row_head == j, vt, jnp.ones_like(vt))


def _pv_t(vt_ref, first_tile, p, j, with_ones):
    out = None
    for t in range(p.shape[0] // KV_TILE):
        part = _dot(_vt_tile(vt_ref, first_tile + t, j, with_ones), p[t * KV_TILE:(t + 1) * KV_TILE])
        out = part if out is None else out + part
    return out


def _merge_heads_t(per_head, normalise):
    parts = []
    for j, acc in enumerate(per_head):
        own = acc[j * HEAD_DIM:(j + 1) * HEAD_DIM]
        if normalise:
            other = (1 - j) * HEAD_DIM
            own = own / acc[other:other + 1]
        parts.append(own)
    return jnp.concatenate(parts, axis=0).T.astype(BF16)


def _prompt_specs(s, tq, batch_first=True):
    nq = s // tq
    if batch_first:
        q_map = lambda b, hp, qi: (b * nq + qi, hp)
        k_map = lambda b, hp, qi: (b, hp)
        vt_map = lambda b, hp, qi: (b, hp, 0)
    else:
        q_map = lambda hp, b, qi: (b * nq + qi, hp)
        k_map = lambda hp, b, qi: (b, hp)
        vt_map = lambda hp, b, qi: (b, hp, 0)
    return (pl.BlockSpec((tq, LANE), q_map), pl.BlockSpec((s, LANE), k_map),
            pl.BlockSpec((s // KV_TILE, LANE, KV_TILE), vt_map))


def _fox_prompt_kernel(q_ref, k_ref, vt_ref, cum_ref, o_ref, cumrep_ref, m_ref, acc_ref, s_ref, p_ref):
    hp, qi = pl.program_id(1), pl.program_id(2)
    tq = q_ref.shape[0]
    reps = tq // LANE

    @pl.when(qi == 0)
    def _():
        c = cum_ref[...]
        lane = lax.broadcasted_iota(jnp.int32, c.shape, 1)
        for j in range(HEADS_PER_STEP):
            col = jnp.sum(jnp.where(lane == hp * HEADS_PER_STEP + j, c, 0.0), axis=1, keepdims=True)
            cumrep_ref[j] = jnp.broadcast_to(col, c.shape)

    q_pad = _pair_operands(q_ref[...])
    m_ref[...] = jnp.full(m_ref.shape, NEG_INF, F32)
    acc_ref[...] = jnp.zeros(acc_ref.shape, F32)
    visible = (lax.broadcasted_iota(jnp.int32, (tq, tq), 0) <= lax.broadcasted_iota(jnp.int32, (tq, tq), 1))

    def step(kb, diagonal):
        ks = pl.multiple_of(kb * tq, tq)
        k = k_ref[pl.ds(ks, tq), :]
        stats = []
        for j in range(HEADS_PER_STEP):
            s = _dot_nt(k, q_pad[j])
            s = s - jnp.concatenate([cumrep_ref[j, pl.ds(ks, tq), :]] * reps, axis=1)
            if diagonal:
                s = jnp.where(visible, s, NEG_INF)
            s_ref[j] = s
            m_old = m_ref[j]
            m_new = jnp.maximum(m_old, jnp.max(s, axis=0, keepdims=True))
            m_ref[j] = m_new
            stats.append((m_new, jnp.exp(m_old - m_new)))
        for j in range(HEADS_PER_STEP):
            p_ref[j] = jnp.exp(s_ref[j] - stats[j][0]).astype(BF16)
        for j in range(HEADS_PER_STEP):
            acc_ref[j] = stats[j][1] * acc_ref[j] + _pv_t(vt_ref, kb * (tq // KV_TILE), p_ref[j], j, True)

    def body(kb, carry):
        step(kb, False)
        return carry

    lax.fori_loop(0, qi, body, 0)
    step(qi, True)
    o_ref[...] = _merge_heads_t([acc_ref[j] for j in range(HEADS_PER_STEP)], True)


def _fox_prompt(q, k, vt, cum, b, s):
    tq = CAUSAL_TILE
    q_spec, k_spec, vt_spec = _prompt_specs(s, tq)
    cum_spec = pl.BlockSpec((s, LANE), lambda bi, hp, qi: (bi, 0))
    return pl.pallas_call(
        _fox_prompt_kernel,
        grid=(b, N_HEAD_PAIRS, s // tq),
        in_specs=[q_spec, k_spec, vt_spec, cum_spec],
        out_specs=q_spec,
        out_shape=jax.ShapeDtypeStruct(q.shape, BF16),
        scratch_shapes=[pltpu.VMEM((HEADS_PER_STEP, s, LANE), F32),
                        pltpu.VMEM((HEADS_PER_STEP, 1, tq), F32),
                        pltpu.VMEM((HEADS_PER_STEP, LANE, tq), F32),
                        pltpu.VMEM((HEADS_PER_STEP, tq, tq), F32),
                        pltpu.VMEM((HEADS_PER_STEP, tq, tq), BF16)],
        compiler_params=_params("parallel", "parallel", "arbitrary"),
        name="fox_prompt",
    )(q, k, vt, cum)


def _stick_prompt_kernel(q_ref, k_ref, vt_ref, o_ref, later_ref, acc_ref, z_ref, lb_ref, x_ref, first_ref,
                         tail_ref, w_ref):
    qi = pl.program_id(2)
    tq = q_ref.shape[0]
    q_pad = _pair_operands(q_ref[...])
    later_ref[...] = jnp.zeros(later_ref.shape, F32)
    acc_ref[...] = jnp.zeros(acc_ref.shape, F32)
    earlier = lax.broadcasted_iota(jnp.int32, (tq, tq), 0) < lax.broadcasted_iota(jnp.int32, (tq, tq), 1)
    after = jnp.where(lax.broadcasted_iota(jnp.int32, (KV_TILE, KV_TILE), 1)
                      > lax.broadcasted_iota(jnp.int32, (KV_TILE, KV_TILE), 0), 1.0, 0.0).astype(BF16)
    after2 = jnp.concatenate([after, after], axis=1)

    n_kv = tq // KV_TILE

    def step(kb, diagonal):
        ks = pl.multiple_of(kb * tq, tq)
        k = k_ref[pl.ds(ks, tq), :]
        for j in range(HEADS_PER_STEP):
            z_ref[j] = _dot_nt(k, q_pad[j])
        for j in range(HEADS_PER_STEP):
            z = z_ref[j]
            log_beta = _log_sigmoid(z)
            log_keep = log_beta - z
            if diagonal:
                log_keep = jnp.where(earlier, log_keep, 0.0)
            lb_ref[j] = log_beta
            for t in range(n_kv):
                hi, lo = _split2(log_keep[t * KV_TILE:(t + 1) * KV_TILE])
                x_ref[j, t, :KV_TILE, :] = hi
                x_ref[j, t, KV_TILE:, :] = lo
                first_ref[j, t] = log_keep[t * KV_TILE:t * KV_TILE + 1]
        for j in range(HEADS_PER_STEP):
            later = later_ref[j]
            for t in reversed(range(n_kv)):
                inner = _dot(after2, x_ref[j, t])
                rows = slice(t * KV_TILE, (t + 1) * KV_TILE)
                tail_ref[j, rows, :] = inner + later
                later = later + (inner[:1] + first_ref[j, t])
            later_ref[j] = later
        for j in range(HEADS_PER_STEP):
            w = jnp.exp(lb_ref[j] + tail_ref[j])
            if diagonal:
                w = jnp.where(earlier, w, 0.0)
            w_ref[j] = w.astype(BF16)
        for j in range(HEADS_PER_STEP):
            acc_ref[j] = acc_ref[j] + _pv_t(vt_ref, kb * n_kv, w_ref[j], j, False)

    step(qi, True)

    def body(t, carry):
        step(qi - 1 - t, False)
        return carry

    lax.fori_loop(0, qi, body, 0)
    o_ref[...] = _merge_heads_t([acc_ref[j] for j in range(HEADS_PER_STEP)], False)


def _stick_prompt(q, k, vt, b, s):
    tq = CAUSAL_TILE
    q_spec, k_spec, vt_spec = _prompt_specs(s, tq)
    return pl.pallas_call(
        _stick_prompt_kernel,
        grid=(b, N_HEAD_PAIRS, s // tq),
        in_specs=[q_spec, k_spec, vt_spec],
        out_specs=q_spec,
        out_shape=jax.ShapeDtypeStruct(q.shape, BF16),
        scratch_shapes=[pltpu.VMEM((HEADS_PER_STEP, 1, tq), F32),
                        pltpu.VMEM((HEADS_PER_STEP, LANE, tq), F32),
                        pltpu.VMEM((HEADS_PER_STEP, tq, tq), F32),
                        pltpu.VMEM((HEADS_PER_STEP, tq, tq), F32),
                        pltpu.VMEM((HEADS_PER_STEP, tq // KV_TILE, 2 * KV_TILE, tq), BF16),
                        pltpu.VMEM((HEADS_PER_STEP, tq // KV_TILE, 1, tq), F32),
                        pltpu.VMEM((HEADS_PER_STEP, tq, tq), F32),
                        pltpu.VMEM((HEADS_PER_STEP, tq, tq), BF16)],
        compiler_params=_params("parallel", "parallel", "arbitrary"),
        name="stick_prompt",
    )(q, k, vt)


def _band_prompt_kernel(q_ref, k_ref, vt_ref, base_ref, o_ref, bias_ref, s_ref, p_ref):
    b, qg = pl.program_id(1), pl.program_id(2)
    tq = BAND_Q_TILE
    n_sub = q_ref.shape[0] // tq

    @pl.when((b == 0) & (qg == 0))
    def _():
        i = lax.broadcasted_iota(jnp.int32, (tq, BAND_KEYS), 0) // CHUNK
        m = lax.broadcasted_iota(jnp.int32, (tq, BAND_KEYS), 1) // CHUNK
        in_band = (m >= i) & (m <= i + LEFT_CHUNKS)
        for j in range(HEADS_PER_STEP):
            rows = jnp.broadcast_to(base_ref[0, j:j + 1, :], (tq, BIAS_WIDTH))
            rolled = pltpu.roll(rows, 0, 1, stride=1, stride_axis=0)
            bias_ref[j] = jnp.where(in_band, rolled[:, :BAND_KEYS], NEG_INF).T

    def attend(tiles):
        maxima = []
        for sub, (first_tile, n_tiles) in enumerate(tiles):
            q_pad = _pair_operands(q_ref[sub * tq:(sub + 1) * tq, :])
            width = n_tiles * KV_TILE
            start = first_tile * KV_TILE
            if not isinstance(start, int):
                start = pl.multiple_of(start, KV_TILE)
            k = k_ref[pl.ds(start, width), :]
            for j in range(HEADS_PER_STEP):
                s = _dot_nt(k, q_pad[j]) + bias_ref[j, BAND_KEYS - width:, :]
                s_ref[sub * HEADS_PER_STEP + j, :width, :] = s
                maxima.append(jnp.max(s, axis=0, keepdims=True))
        for sub, (first_tile, n_tiles) in enumerate(tiles):
            width = n_tiles * KV_TILE
            for j in range(HEADS_PER_STEP):
                c = sub * HEADS_PER_STEP + j
                p_ref[c, :width, :] = jnp.exp(s_ref[c, :width, :] - maxima[c]).astype(BF16)
        for sub, (first_tile, n_tiles) in enumerate(tiles):
            width = n_tiles * KV_TILE
            per_head = [_pv_t(vt_ref, first_tile, p_ref[sub * HEADS_PER_STEP + j, :width, :], j, True)
                        for j in range(HEADS_PER_STEP)]
            o_ref[sub * tq:(sub + 1) * tq, :] = _merge_heads_t(per_head, True)

    n_lead = A_WINDOW // tq
    band_tiles = BAND_KEYS // KV_TILE
    n_lead_steps = -(-n_lead // n_sub)
    for g in range(n_lead_steps):
        lead = [(0, t + 1) if t < n_lead else (t - n_lead, band_tiles) for t in range(g * n_sub, (g + 1) * n_sub)]
        pl.when(qg == g)(functools.partial(attend, lead))

    @pl.when(qg >= n_lead_steps)
    def _():
        attend([(qg * n_sub + sub - n_lead, band_tiles) for sub in range(n_sub)])


def _band_prompt(q, k, vt, base, b, s):
    tq = BAND_Q_TILE * BAND_TILES_PER_STEP
    q_spec, k_spec, vt_spec = _prompt_specs(s, tq, batch_first=False)
    base_spec = pl.BlockSpec((1, HEADS_PER_STEP, BIAS_WIDTH), lambda hp, bi, qi: (hp, 0, 0))
    return pl.pallas_call(
        _band_prompt_kernel,
        grid=(N_HEAD_PAIRS, b, s // tq),
        in_specs=[q_spec, k_spec, vt_spec, base_spec],
        out_specs=q_spec,
        out_shape=jax.ShapeDtypeStruct(q.shape, BF16),
        scratch_shapes=[pltpu.VMEM((HEADS_PER_STEP, BAND_KEYS, BAND_Q_TILE), F32),
                        pltpu.VMEM((BAND_TILES_PER_STEP * HEADS_PER_STEP, BAND_KEYS, BAND_Q_TILE), F32),
                        pltpu.VMEM((BAND_TILES_PER_STEP * HEADS_PER_STEP, BAND_KEYS, BAND_Q_TILE), BF16)],
        compiler_params=_params("arbitrary", "arbitrary", "arbitrary"),
        name="band_prompt",
    )(q, k, vt, base)


def _rel_rows_desc(rel_table, hi, lo):
    top = jnp.broadcast_to(rel_table[-1], (max(0, hi - max(lo, REL_MAX + 1) + 1), rel_table.shape[1]))
    mid = rel_table[max(lo, -REL_MAX) + REL_MAX:min(hi, REL_MAX) + REL_MAX + 1][::-1]
    bot = jnp.broadcast_to(rel_table[0], (max(0, min(hi, -REL_MAX - 1) - lo + 1), rel_table.shape[1]))
    return jnp.concatenate([top, mid, bot], axis=0)


def _band_base(rel_table, q_rows, n_keys, width):
    offset = n_keys - q_rows
    desc = _rel_rows_desc(rel_table, offset + q_rows - 1, offset - n_keys + 1)
    pad = jnp.zeros((width - desc.shape[0], rel_table.shape[1]), rel_table.dtype)
    return jnp.concatenate([desc[q_rows - 1:], pad, desc[:q_rows - 1]], axis=0).T


def _sample_bias(rel_table, l, n_cache):
    n = n_cache + l
    desc = _rel_rows_desc(rel_table, n_cache + l - 1, -(l - 1))
    cols = jnp.stack([desc[l - 1 - i:l - 1 - i + n] for i in range(l)], axis=1)
    return jnp.swapaxes(cols, 1, 2).reshape(n, rel_table.shape[1] * l)


def _block_diag_q(q, l):
    d = q.shape[1]
    tiled = jnp.concatenate([q] * N_HEADS, axis=0)
    r = lax.broadcasted_iota(jnp.int32, (N_HEADS * l, d), 0) // l
    c = lax.broadcasted_iota(jnp.int32, (N_HEADS * l, d), 1) // HEAD_DIM
    return jnp.where(r == c, tiled, jnp.zeros_like(tiled))


def _gather_heads(r, l):
    d = r.shape[1]
    rr = lax.broadcasted_iota(jnp.int32, (N_HEADS * l, d), 0) // l
    c = lax.broadcasted_iota(jnp.int32, (N_HEADS * l, d), 1) // HEAD_DIM
    r = jnp.where(rr == c, r, 0.0)
    out = r[:l]
    for h in range(1, N_HEADS):
        out = out + r[h * l:(h + 1) * l]
    return out


def _new_key_masks(l, strict):
    j = lax.broadcasted_iota(jnp.int32, (l, N_HEADS * l), 0)
    i = lax.broadcasted_iota(jnp.int32, (l, N_HEADS * l), 1) % l
    return (j < i) if strict else (j <= i)


def _band_sample_kernel(q_ref, kn_ref, vn_ref, kc_ref, vc_ref, bias_ref, o_ref):
    l = q_ref.shape[0]
    qb = _block_diag_q(q_ref[...], l)
    kk = jnp.concatenate([kc_ref[0].astype(BF16), kn_ref[...]], axis=0)
    vv = jnp.concatenate([vc_ref[0].astype(BF16), vn_ref[...]], axis=0)
    s = _dot_nt(kk, qb) + bias_ref[...]
    p = jnp.exp(s - jnp.max(s, axis=0, keepdims=True))
    p = p / jnp.sum(p, axis=0, keepdims=True)
    o_ref[...] = _gather_heads(_dot_tn(p.astype(BF16), vv), l).astype(BF16)


def _fox_sample_kernel(q_ref, kn_ref, vn_ref, kc_ref, vc_ref, fc_ref, fn_ref, o_ref):
    l = q_ref.shape[0]
    past = kc_ref.shape[1]
    n = past + l
    qb = _block_diag_q(q_ref[...], l)
    kk = jnp.concatenate([kc_ref[0].astype(BF16), kn_ref[...]], axis=0)
    vv = jnp.concatenate([vc_ref[0].astype(BF16), vn_ref[...]], axis=0)
    log_f = jnp.concatenate([fc_ref[0], fn_ref[:, :N_HEADS]], axis=0)
    eh = lax.broadcasted_iota(jnp.int32, (N_HEADS, N_HEADS * l), 0)
    ec = lax.broadcasted_iota(jnp.int32, (N_HEADS, N_HEADS * l), 1) // l
    expand = jnp.where(eh == ec, 1.0, 0.0).astype(BF16)
    upto = jnp.where(lax.broadcasted_iota(jnp.int32, (n, n), 1)
                     <= lax.broadcasted_iota(jnp.int32, (n, n), 0), 1.0, 0.0).astype(BF16)
    spread = sum(_dot(part, expand) for part in _split3(log_f))
    cum = sum(_dot(upto, part) for part in _split3(spread))
    s = _dot_nt(kk, qb) - cum
    visible = _new_key_masks(l, strict=False)
    s = jnp.concatenate([s[:past], jnp.where(visible, s[past:], NEG_INF)], axis=0)
    p = jnp.exp(s - jnp.max(s, axis=0, keepdims=True))
    p = p / jnp.sum(p, axis=0, keepdims=True)
    o_ref[...] = _gather_heads(_dot_tn(p.astype(BF16), vv), l).astype(BF16)


def _stick_sample_kernel(q_ref, kn_ref, vn_ref, kc_ref, vc_ref, o_ref):
    l = q_ref.shape[0]
    past = kc_ref.shape[1]
    n = past + l
    qb = _block_diag_q(q_ref[...], l)
    kk = jnp.concatenate([kc_ref[0].astype(BF16), kn_ref[...]], axis=0)
    vv = jnp.concatenate([vc_ref[0].astype(BF16), vn_ref[...]], axis=0)
    z = _dot_nt(kk, qb)
    log_beta = _log_sigmoid(z)
    log_keep = log_beta - z
    earlier = _new_key_masks(l, strict=True)
    log_keep = jnp.concatenate([log_keep[:past], jnp.where(earlier, log_keep[past:], 0.0)], axis=0)
    after = jnp.where(lax.broadcasted_iota(jnp.int32, (n, n), 1)
                      > lax.broadcasted_iota(jnp.int32, (n, n), 0), 1.0, 0.0).astype(BF16)
    hi, lo = _split2(log_keep)
    tail = _dot(after, hi) + _dot(after, lo)
    w = jnp.exp(log_beta + tail)
    w = jnp.concatenate([w[:past], jnp.where(earlier, w[past:], 0.0)], axis=0)
    o_ref[...] = _gather_heads(_dot_tn(w.astype(BF16), vv), l).astype(BF16)


def _sample_call(kernel, name, q, k_new, v_new, cache_k, cache_v, extra=(), extra_specs=()):
    nb, past, d = cache_k.shape
    l = q.shape[0] // nb
    new = pl.BlockSpec((l, d), lambda b: (b, 0))
    cache = pl.BlockSpec((1, past, d), lambda b: (b, 0, 0))
    return pl.pallas_call(
        kernel,
        grid=(nb,),
        in_specs=[new, new, new, cache, cache, *extra_specs],
        out_specs=new,
        out_shape=jax.ShapeDtypeStruct(q.shape, BF16),
        compiler_params=_params("parallel"),
        name=name,
    )(q, k_new, v_new, cache_k, cache_v, *extra)


def _cumsum_kernel(f_ref, o_ref):
    upto = jnp.where(lax.broadcasted_iota(jnp.int32, (KV_TILE, KV_TILE), 1)
                     <= lax.broadcasted_iota(jnp.int32, (KV_TILE, KV_TILE), 0), 1.0, 0.0).astype(BF16)
    carry = jnp.zeros((1, f_ref.shape[1]), F32)
    for t in range(f_ref.shape[0] // KV_TILE):
        rows = slice(t * KV_TILE, (t + 1) * KV_TILE)
        c = sum(_dot(upto, part) for part in _split3(f_ref[rows, :])) + carry
        o_ref[rows, :] = c
        carry = c[KV_TILE - 1:, :]


def _cumsum_rows(log_f, s):
    spec = pl.BlockSpec((s, log_f.shape[1]), lambda i: (i, 0))
    return pl.pallas_call(
        _cumsum_kernel, grid=(log_f.shape[0] // s,), in_specs=[spec], out_specs=spec,
        out_shape=jax.ShapeDtypeStruct(log_f.shape, F32),
        compiler_params=_params("parallel"), name="forget_cumsum",
    )(log_f)


def kernel(x_prompt, x_sample, cache_a_k, cache_a_v, cache_b_k, cache_b_v, cache_b_logf, cache_c_k, cache_c_v, norm_ffn1, ffn1_gate, ffn1_up, ffn1_down, norm_mix, norm_ffn2, ffn2_gate, ffn2_up, ffn2_down, a_w_qkv, a_w_o, a_rel_bias, b_w_qkv, b_w_o, b_w_f, b_b_f, c_w_qkv, c_w_o, norm_final):
    b, s, d = x_prompt.shape
    nb, l, _ = x_sample.shape
    depth = norm_ffn1.shape[0]
    assert d == N_HEADS * HEAD_DIM and s % CAUSAL_TILE == 0 and s >= A_WINDOW and A_WINDOW % BAND_Q_TILE == 0

    xp = x_prompt.reshape(b * s, d)
    xs = x_sample.reshape(nb * l, d)
    heads = (N_HEADS, HEAD_DIM)
    order = ("a_kp", "a_vp", "a_ks", "a_vs", "b_kp", "b_vp", "b_fp", "b_ks", "b_vs", "b_fs",
             "c_kp", "c_vp", "c_ks", "c_vs")
    outs = {name: [] for name in order}
    for i in range(depth):
        kind, slot = i % 3, i // 3
        w1 = (ffn1_gate[i].astype(BF16), ffn1_up[i].astype(BF16), ffn1_down[i].astype(BF16))
        w2 = (ffn2_gate[i].astype(BF16), ffn2_up[i].astype(BF16), ffn2_down[i].astype(BF16))
        xp = _ffn(xp, norm_ffn1[i], *w1)
        xs = _ffn(xs, norm_ffn1[i], *w1)
        w_qkv, w_o = ((a_w_qkv, a_w_o), (b_w_qkv, b_w_o), (c_w_qkv, c_w_o))[kind]
        w_qkv = w_qkv[slot].astype(BF16)
        w_o = w_o[slot].astype(BF16)
        forget = None
        if kind == 1:
            forget = (jnp.pad(b_w_f[slot], ((0, 0), (0, LANE - N_HEADS))).astype(BF16),
                      jnp.pad(b_b_f[slot], (0, LANE - N_HEADS)).reshape(1, LANE))
        qp, kp, vp, kpb, vpt, *fp = _qkv(xp, norm_mix[i], w_qkv, forget, transposed_v=True)
        qs, ks, vs, ksb, vsb, *fs = _qkv(xs, norm_mix[i], w_qkv, forget)
        if kind == 0:
            base = _band_base(a_rel_bias[slot], BAND_Q_TILE, BAND_KEYS, BIAS_WIDTH)
            base = base.reshape(N_HEAD_PAIRS, HEADS_PER_STEP, BIAS_WIDTH)
            op = _band_prompt(qp, kpb, vpt, base, b, s)
            n_cache = cache_a_k.shape[2]
            bias_s = _sample_bias(a_rel_bias[slot], l, n_cache)
            os_ = _sample_call(_band_sample_kernel, "band_sample", qs, ksb, vsb,
                               cache_a_k[slot].reshape(nb, n_cache, d), cache_a_v[slot].reshape(nb, n_cache, d),
                               extra=(bias_s,), extra_specs=(_resident(bias_s.shape),))
            keep = min(A_WINDOW, s)
            outs["a_kp"].append(kp.reshape(b, s, *heads)[:, s - keep:])
            outs["a_vp"].append(vp.reshape(b, s, *heads)[:, s - keep:])
            outs["a_ks"].append(ks.reshape(nb, l, *heads))
            outs["a_vs"].append(vs.reshape(nb, l, *heads))
        elif kind == 1:
            op = _fox_prompt(qp, kpb, vpt, _cumsum_rows(fp[0], s), b, s)
            past = cache_b_k.shape[2]
            os_ = _sample_call(_fox_sample_kernel, "fox_sample", qs, ksb, vsb,
                               cache_b_k[slot].reshape(nb, past, d), cache_b_v[slot].reshape(nb, past, d),
                               extra=(cache_b_logf[slot], fs[0]),
                               extra_specs=(pl.BlockSpec((1, past, N_HEADS), lambda bi: (bi, 0, 0)),
                                            pl.BlockSpec((l, LANE), lambda bi: (bi, 0))))
            outs["b_kp"].append(kp.reshape(b, s, *heads))
            outs["b_vp"].append(vp.reshape(b, s, *heads))
            outs["b_fp"].append(fp[0][:, :N_HEADS].reshape(b, s, N_HEADS))
            outs["b_ks"].append(ks.reshape(nb, l, *heads))
            outs["b_vs"].append(vs.reshape(nb, l, *heads))
            outs["b_fs"].append(fs[0][:, :N_HEADS].reshape(nb, l, N_HEADS))
        else:
            op = _stick_prompt(qp, kpb, vpt, b, s)
            past = cache_c_k.shape[2]
            os_ = _sample_call(_stick_sample_kernel, "stick_sample", qs, ksb, vsb,
                               cache_c_k[slot].reshape(nb, past, d), cache_c_v[slot].reshape(nb, past, d))
            outs["c_kp"].append(kp.reshape(b, s, *heads))
            outs["c_vp"].append(vp.reshape(b, s, *heads))
            outs["c_ks"].append(ks.reshape(nb, l, *heads))
            outs["c_vs"].append(vs.reshape(nb, l, *heads))
        final = norm_final if i == depth - 1 else None
        xp = _ffn(xp, norm_ffn2[i], *w2, proj=(op, w_o), final_g=final)
        xs = _ffn(xs, norm_ffn2[i], *w2, proj=(os_, w_o), final_g=final)

    return (xp.reshape(b, s, d), xs.reshape(nb, l, d)) + tuple(jnp.stack(outs[name]) for name in order)
```

```python
import functools

import jax
import jax.numpy as jnp
from jax import lax
from jax.experimental import pallas as pl
from jax.experimental.pallas import tpu as pltpu

F32 = jnp.float32
BF16 = jnp.bfloat16

N_HEADS = 16
HEAD_DIM = 64
CHUNK = 64
LEFT_CHUNKS = 8
A_WINDOW = LEFT_CHUNKS * CHUNK
REL_MAX = 256
RMS_EPS = 1e-6
ATTN_SCALE = HEAD_DIM ** -0.5
NEG_INF = -1e30

LANE = 128
HEADS_PER_STEP = LANE // HEAD_DIM
N_HEAD_PAIRS = N_HEADS // HEADS_PER_STEP
VMEM_LIMIT = 56 * 1024 * 1024

ROW_TILE = 512
FF_TILE = 256
KV_TILE = 256
CAUSAL_TILE = 512
BAND_Q_TILE = 256
BAND_TILES_PER_STEP = 4
BAND_KEYS = BAND_Q_TILE + A_WINDOW
BIAS_WIDTH = 1024


def _params(*sem):
    return pltpu.CompilerParams(dimension_semantics=sem, vmem_limit_bytes=VMEM_LIMIT)


def _rms(x, g):
    y = x * lax.rsqrt(jnp.mean(x * x, axis=-1, keepdims=True) + RMS_EPS)
    return y * g


def _log_sigmoid(z):
    return jnp.minimum(z, 0.0) - jnp.log(1.0 + jnp.exp(-jnp.abs(z)))


def _dot(a, b):
    return jnp.dot(a, b, preferred_element_type=F32)


def _dot_nt(a, b):
    return lax.dot_general(a, b, (((1,), (1,)), ((), ())), preferred_element_type=F32)


def _dot_tn(a, b):
    return lax.dot_general(a, b, (((0,), (0,)), ((), ())), preferred_element_type=F32)


def _split2(x):
    hi = x.astype(BF16)
    lo = (x - hi.astype(F32)).astype(BF16)
    return hi, lo


def _split3(x):
    hi, lo = _split2(x)
    lo2 = (x - hi.astype(F32) - lo.astype(F32)).astype(BF16)
    return hi, lo, lo2


def _resident(shape):
    return pl.BlockSpec(shape, lambda *_: (0,) * len(shape), pipeline_mode=pl.Buffered(1))


def _resident_layer(shape, layer):
    return pl.BlockSpec((None,) + tuple(shape[1:]), lambda *_: (layer,) + (0,) * (len(shape) - 1),
                        pipeline_mode=pl.Buffered(1))


def _ffn_kernel(*refs, has_proj, has_final):
    refs = list(refs)
    x_ref = refs.pop(0)
    if has_proj:
        o_ref, wo_ref = refs.pop(0), refs.pop(0)
    g_ref, wg_ref, wu_ref, wd_ref = refs[:4]
    refs = refs[4:]
    if has_final:
        gf_ref = refs.pop(0)
    out_ref = refs.pop(0)

    x = x_ref[...]
    if has_proj:
        x = x + _dot(o_ref[...], wo_ref[...])
    h = _rms(x, g_ref[...]).astype(BF16)
    d_ff = wg_ref.shape[1]
    acc = jnp.zeros(x.shape, F32)
    for c in range(d_ff // FF_TILE):
        sl = slice(c * FF_TILE, (c + 1) * FF_TILE)
        gate = _dot(h, wg_ref[:, sl])
        up = _dot(h, wu_ref[:, sl])
        act = (gate * jax.nn.sigmoid(gate) * up).astype(BF16)
        acc = acc + _dot(act, wd_ref[sl, :])
    y = x + 0.5 * acc
    if has_final:
        y = _rms(y, gf_ref[...])
    out_ref[...] = y


def _ffn(x, layer, g, wg, wu, wd, proj=None, final_g=None):
    n, d = x.shape
    d_ff = wg.shape[2]
    tm = min(ROW_TILE, n)
    assert n % tm == 0 and d_ff % FF_TILE == 0
    row = pl.BlockSpec((tm, d), lambda i: (i, 0))
    args, specs = [x], [row]
    if proj is not None:
        o, w_o, slot = proj
        args += [o, w_o]
        specs += [row, _resident_layer(w_o.shape, slot)]
    g = g.reshape(g.shape[0], 1, d)
    args += [g, wg, wu, wd]
    specs += [_resident_layer(a.shape, layer) for a in (g, wg, wu, wd)]
    if final_g is not None:
        args.append(final_g.reshape(1, d))
        specs.append(_resident((1, d)))
    return pl.pallas_call(
        functools.partial(_ffn_kernel, has_proj=proj is not None, has_final=final_g is not None),
        grid=(n // tm,),
        in_specs=specs,
        out_specs=row,
        out_shape=jax.ShapeDtypeStruct((n, d), F32),
        compiler_params=_params("parallel"),
        name="ffn",
    )(*args)


def _qkv_kernel(*refs, has_forget, transposed_v, has_keep):
    x_ref, g_ref, w_ref = refs[:3]
    refs = list(refs[3:])
    if has_forget:
        wf_ref, bf_ref = refs[:2]
        refs = refs[2:]
    q_ref, k_ref, v_ref, kb_ref, vb_ref = refs[:5]
    refs = refs[5:]
    d = x_ref.shape[1]
    h = _rms(x_ref[...], g_ref[...]).astype(BF16)
    q_ref[...] = (_dot(h, w_ref[:, :d]) * ATTN_SCALE).astype(BF16)
    k = _dot(h, w_ref[:, d:2 * d])
    k4 = k.reshape(k_ref.shape)
    k_ref[...] = k4
    kb_ref[...] = k.astype(BF16)
    v = _dot(h, w_ref[:, 2 * d:])
    v4 = v.reshape(v_ref.shape)
    v_ref[...] = v4
    if transposed_v:
        vt = v.T.astype(BF16)
        for t in range(vb_ref.shape[0]):
            vb_ref[t] = vt[:, t * KV_TILE:(t + 1) * KV_TILE]
    else:
        vb_ref[...] = v.astype(BF16)
    if has_forget:
        refs.pop(0)[...] = _log_sigmoid(_dot(h, wf_ref[...]) + bf_ref[...])
    if has_keep:
        refs[0][...] = k4
        refs[1][...] = v4


def _qkv(x, layer, slot, g, w, forget=None, transposed_v=False, keep_last=None):
    n, d = x.shape
    tm = min(ROW_TILE, n)
    assert n % tm == 0 and tm % KV_TILE == 0
    row = pl.BlockSpec((tm, d), lambda i: (i, 0))
    cache_row = pl.BlockSpec((tm, N_HEADS, HEAD_DIM), lambda i: (i, 0, 0))
    g = g.reshape(g.shape[0], 1, d)
    args = [x, g, w]
    specs = [row, _resident_layer(g.shape, layer), _resident_layer(w.shape, slot)]
    if transposed_v:
        vb_shape = jax.ShapeDtypeStruct((n // KV_TILE, d, KV_TILE), BF16)
        vb_spec = pl.BlockSpec((tm // KV_TILE, d, KV_TILE), lambda i: (i, 0, 0))
    else:
        vb_shape, vb_spec = jax.ShapeDtypeStruct((n, d), BF16), row
    shapes = [jax.ShapeDtypeStruct((n, d), BF16), jax.ShapeDtypeStruct((n, N_HEADS, HEAD_DIM), F32),
              jax.ShapeDtypeStruct((n, N_HEADS, HEAD_DIM), F32), jax.ShapeDtypeStruct((n, d), BF16), vb_shape]
    out_specs = [row, cache_row, cache_row, row, vb_spec]
    if forget is not None:
        args += list(forget)
        specs += [_resident_layer(a.shape, slot) for a in forget]
        shapes.append(jax.ShapeDtypeStruct((n, LANE), F32))
        out_specs.append(pl.BlockSpec((tm, LANE), lambda i: (i, 0)))
    if keep_last is not None:
        seq = keep_last
        assert seq % tm == 0 and n % seq == 0
        keep_spec = pl.BlockSpec((None, tm, N_HEADS, HEAD_DIM), lambda i: (i // (seq // tm), 0, 0, 0))
        shapes += [jax.ShapeDtypeStruct((n // seq, tm, N_HEADS, HEAD_DIM), F32)] * 2
        out_specs += [keep_spec, keep_spec]
    return pl.pallas_call(
        functools.partial(_qkv_kernel, has_forget=forget is not None, transposed_v=transposed_v,
                          has_keep=keep_last is not None),
        grid=(n // tm,),
        in_specs=specs,
        out_specs=out_specs,
        out_shape=shapes,
        compiler_params=_params("arbitrary" if keep_last is not None else "parallel"),
        name="qkv",
    )(*args)


def _pair_operands(q):
    lane_head = lax.broadcasted_iota(jnp.int32, q.shape, 1) // HEAD_DIM
    return [jnp.where(lane_head == j, q, jnp.zeros_like(q)) for j in range(HEADS_PER_STEP)]


def _vt_tile(vt_ref, t, j, with_ones):
    vt = vt_ref[t]
    if not with_ones:
        return vt
    row_head = lax.broadcasted_iota(jnp.int32, vt.shape, 0) // HEAD_DIM
    return jnp.where(row_head == j, vt, jnp.ones_like(vt))


def _pv_t(vt_ref, first_tile, p, j, with_ones):
    out = None
    for t in range(p.shape[0] // KV_TILE):
        part = _dot(_vt_tile(vt_ref, first_tile + t, j, with_ones), p[t * KV_TILE:(t + 1) * KV_TILE])
        out = part if out is None else out + part
    return out


def _merge_heads_t(per_head, normalise):
    parts = []
    for j, acc in enumerate(per_head):
        own = acc[j * HEAD_DIM:(j + 1) * HEAD_DIM]
        if normalise:
            other = (1 - j) * HEAD_DIM
            own = own / acc[other:other + 1]
        parts.append(own)
    return jnp.concatenate(parts, axis=0).T.astype(BF16)


def _causal_specs(s):
    seq = pl.BlockSpec((s, LANE), lambda b, hp: (b, hp))
    return seq, pl.BlockSpec((s // KV_TILE, LANE, KV_TILE), lambda b, hp: (b, hp, 0))


def _fox_prompt_kernel(q_ref, k_ref, vt_ref, cum_ref, o_ref, cumrep_ref, m_ref, acc_ref, s_ref, p_ref):
    hp = pl.program_id(1)
    tq = CAUSAL_TILE
    reps = tq // LANE
    n_kv = tq // KV_TILE
    c = cum_ref[...]
    lane = lax.broadcasted_iota(jnp.int32, c.shape, 1)
    for j in range(HEADS_PER_STEP):
        col = jnp.sum(jnp.where(lane == hp * HEADS_PER_STEP + j, c, 0.0), axis=1, keepdims=True)
        cumrep_ref[j] = jnp.broadcast_to(col, c.shape)
    visible = (lax.broadcasted_iota(jnp.int32, (tq, tq), 0) <= lax.broadcasted_iota(jnp.int32, (tq, tq), 1))
    pair_no = 0
    for qi in range(q_ref.shape[0] // tq):
        q_rows = slice(qi * tq, (qi + 1) * tq)
        q_pad = _pair_operands(q_ref[q_rows, :])
        m_ref[...] = jnp.full(m_ref.shape, NEG_INF, F32)
        acc_ref[...] = jnp.zeros(acc_ref.shape, F32)
        for kb in range(qi + 1):
            slot = pair_no % 2
            pair_no += 1
            k_rows = slice(kb * tq, (kb + 1) * tq)
            k = k_ref[k_rows, :]
            stats = []
            for j in range(HEADS_PER_STEP):
                s = _dot_nt(k, q_pad[j])
                s = s - jnp.concatenate([cumrep_ref[j, k_rows, :]] * reps, axis=1)
                if kb == qi:
                    s = jnp.where(visible, s, NEG_INF)
                s_ref[slot, j] = s
                m_old = m_ref[j]
                m_new = jnp.maximum(m_old, jnp.max(s, axis=0, keepdims=True))
                m_ref[j] = m_new
                stats.append((m_new, jnp.exp(m_old - m_new)))
            for j in range(HEADS_PER_STEP):
                p_ref[slot, j] = jnp.exp(s_ref[slot, j] - stats[j][0]).astype(BF16)
            for j in range(HEADS_PER_STEP):
                acc_ref[j] = stats[j][1] * acc_ref[j] + _pv_t(vt_ref, kb * n_kv, p_ref[slot, j], j, True)
        o_ref[q_rows, :] = _merge_heads_t([acc_ref[j] for j in range(HEADS_PER_STEP)], True)


def _fox_prompt(q, k, vt, cum, b, s):
    tq = CAUSAL_TILE
    seq, vt_spec = _causal_specs(s)
    cum_spec = pl.BlockSpec((s, LANE), lambda bi, hp: (bi, 0))
    return pl.pallas_call(
        _fox_prompt_kernel,
        grid=(b, N_HEAD_PAIRS),
        in_specs=[seq, seq, vt_spec, cum_spec],
        out_specs=seq,
        out_shape=jax.ShapeDtypeStruct(q.shape, BF16),
        scratch_shapes=[pltpu.VMEM((HEADS_PER_STEP, s, LANE), F32),
                        pltpu.VMEM((HEADS_PER_STEP, 1, tq), F32),
                        pltpu.VMEM((HEADS_PER_STEP, LANE, tq), F32),
                        pltpu.VMEM((2, HEADS_PER_STEP, tq, tq), F32),
                        pltpu.VMEM((2, HEADS_PER_STEP, tq, tq), BF16)],
        compiler_params=_params("parallel", "parallel"),
        name="fox_prompt",
    )(q, k, vt, cum)


def _stick_prompt_kernel(q_ref, k_ref, vt_ref, o_ref, later_ref, acc_ref, z_ref, lb_ref, x_ref, first_ref,
                         tail_ref, w_ref):
    tq = CAUSAL_TILE
    n_kv = tq // KV_TILE
    earlier = lax.broadcasted_iota(jnp.int32, (tq, tq), 0) < lax.broadcasted_iota(jnp.int32, (tq, tq), 1)
    after = jnp.where(lax.broadcasted_iota(jnp.int32, (KV_TILE, KV_TILE), 1)
                      > lax.broadcasted_iota(jnp.int32, (KV_TILE, KV_TILE), 0), 1.0, 0.0).astype(BF16)
    after2 = jnp.concatenate([after, after], axis=1)
    pair_no = 0
    for qi in range(q_ref.shape[0] // tq):
        q_rows = slice(qi * tq, (qi + 1) * tq)
        q_pad = _pair_operands(q_ref[q_rows, :])
        later_ref[...] = jnp.zeros(later_ref.shape, F32)
        acc_ref[...] = jnp.zeros(acc_ref.shape, F32)
        for kb in reversed(range(qi + 1)):
            slot = pair_no % 2
            pair_no += 1
            diagonal = kb == qi
            k = k_ref[kb * tq:(kb + 1) * tq, :]
            for j in range(HEADS_PER_STEP):
                z_ref[slot, j] = _dot_nt(k, q_pad[j])
            for j in range(HEADS_PER_STEP):
                z = z_ref[slot, j]
                log_beta = _log_sigmoid(z)
                log_keep = log_beta - z
                if diagonal:
                    log_keep = jnp.where(earlier, log_keep, 0.0)
                lb_ref[slot, j] = log_beta
                for t in range(n_kv):
                    hi, lo = _split2(log_keep[t * KV_TILE:(t + 1) * KV_TILE])
                    x_ref[slot, j, t, :KV_TILE, :] = hi
                    x_ref[slot, j, t, KV_TILE:, :] = lo
                    first_ref[slot, j, t] = log_keep[t * KV_TILE:t * KV_TILE + 1]
            for j in range(HEADS_PER_STEP):
                later = later_ref[j]
                for t in reversed(range(n_kv)):
                    inner = _dot(after2, x_ref[slot, j, t])
                    tail_ref[slot, j, t * KV_TILE:(t + 1) * KV_TILE, :] = inner + later
                    later = later + (inner[:1] + first_ref[slot, j, t])
                later_ref[j] = later
            for j in range(HEADS_PER_STEP):
                w = jnp.exp(lb_ref[slot, j] + tail_ref[slot, j])
                if diagonal:
                    w = jnp.where(earlier, w, 0.0)
                w_ref[slot, j] = w.astype(BF16)
            for j in range(HEADS_PER_STEP):
                acc_ref[j] = acc_ref[j] + _pv_t(vt_ref, kb * n_kv, w_ref[slot, j], j, False)
        o_ref[q_rows, :] = _merge_heads_t([acc_ref[j] for j in range(HEADS_PER_STEP)], False)


def _stick_prompt(q, k, vt, b, s):
    tq = CAUSAL_TILE
    seq, vt_spec = _causal_specs(s)
    stage = (2, HEADS_PER_STEP)
    return pl.pallas_call(
        _stick_prompt_kernel,
        grid=(b, N_HEAD_PAIRS),
        in_specs=[seq, seq, vt_spec],
        out_specs=seq,
        out_shape=jax.ShapeDtypeStruct(q.shape, BF16),
        scratch_shapes=[pltpu.VMEM((HEADS_PER_STEP, 1, tq), F32),
                        pltpu.VMEM((HEADS_PER_STEP, LANE, tq), F32),
                        pltpu.VMEM(stage + (tq, tq), F32),
                        pltpu.VMEM(stage + (tq, tq), F32),
                        pltpu.VMEM(stage + (tq // KV_TILE, 2 * KV_TILE, tq), BF16),
                        pltpu.VMEM(stage + (tq // KV_TILE, 1, tq), F32),
                        pltpu.VMEM(stage + (tq, tq), F32),
                        pltpu.VMEM(stage + (tq, tq), BF16)],
        compiler_params=_params("parallel", "parallel"),
        name="stick_prompt",
    )(q, k, vt)


def _band_prompt_kernel(q_ref, k_ref, vt_ref, base_ref, o_ref, bias_ref, s_ref, p_ref):
    b, qg = pl.program_id(1), pl.program_id(2)
    tq = BAND_Q_TILE
    n_sub = q_ref.shape[0] // tq

    @pl.when((b == 0) & (qg == 0))
    def _():
        i = lax.broadcasted_iota(jnp.int32, (tq, BAND_KEYS), 0) // CHUNK
        m = lax.broadcasted_iota(jnp.int32, (tq, BAND_KEYS), 1) // CHUNK
        in_band = (m >= i) & (m <= i + LEFT_CHUNKS)
        for j in range(HEADS_PER_STEP):
            rows = jnp.broadcast_to(base_ref[0, j:j + 1, :], (tq, BIAS_WIDTH))
            rolled = pltpu.roll(rows, 0, 1, stride=1, stride_axis=0)
            bias_ref[j] = jnp.where(in_band, rolled[:, :BAND_KEYS], NEG_INF).T

    def attend(tiles):
        maxima = []
        for sub, (first_tile, n_tiles) in enumerate(tiles):
            q_pad = _pair_operands(q_ref[sub * tq:(sub + 1) * tq, :])
            width = n_tiles * KV_TILE
            start = first_tile * KV_TILE
            if not isinstance(start, int):
                start = pl.multiple_of(start, KV_TILE)
            k = k_ref[pl.ds(start, width), :]
            for j in range(HEADS_PER_STEP):
                s = _dot_nt(k, q_pad[j]) + bias_ref[j, BAND_KEYS - width:, :]
                s_ref[sub * HEADS_PER_STEP + j, :width, :] = s
                maxima.append(jnp.max(s, axis=0, keepdims=True))
        for sub, (first_tile, n_tiles) in enumerate(tiles):
            width = n_tiles * KV_TILE
            for j in range(HEADS_PER_STEP):
                c = sub * HEADS_PER_STEP + j
                p_ref[c, :width, :] = jnp.exp(s_ref[c, :width, :] - maxima[c]).astype(BF16)
        for sub, (first_tile, n_tiles) in enumerate(tiles):
            width = n_tiles * KV_TILE
            per_head = [_pv_t(vt_ref, first_tile, p_ref[sub * HEADS_PER_STEP + j, :width, :], j, True)
                        for j in range(HEADS_PER_STEP)]
            o_ref[sub * tq:(sub + 1) * tq, :] = _merge_heads_t(per_head, True)

    n_lead = A_WINDOW // tq
    band_tiles = BAND_KEYS // KV_TILE
    n_lead_steps = -(-n_lead // n_sub)
    for g in range(n_lead_steps):
        lead = [(0, t + 1) if t < n_lead else (t - n_lead, band_tiles) for t in range(g * n_sub, (g + 1) * n_sub)]
        pl.when(qg == g)(functools.partial(attend, lead))

    @pl.when(qg >= n_lead_steps)
    def _():
        attend([(qg * n_sub + sub - n_lead, band_tiles) for sub in range(n_sub)])


def _band_prompt(q, k, vt, base, b, s):
    tq = BAND_Q_TILE * BAND_TILES_PER_STEP
    nq = s // tq
    q_spec = pl.BlockSpec((tq, LANE), lambda hp, bi, qi: (bi * nq + qi, hp))
    k_spec = pl.BlockSpec((s, LANE), lambda hp, bi, qi: (bi, hp))
    vt_spec = pl.BlockSpec((s // KV_TILE, LANE, KV_TILE), lambda hp, bi, qi: (bi, hp, 0))
    base_spec = pl.BlockSpec((1, HEADS_PER_STEP, BIAS_WIDTH), lambda hp, bi, qi: (hp, 0, 0))
    return pl.pallas_call(
        _band_prompt_kernel,
        grid=(N_HEAD_PAIRS, b, s // tq),
        in_specs=[q_spec, k_spec, vt_spec, base_spec],
        out_specs=q_spec,
        out_shape=jax.ShapeDtypeStruct(q.shape, BF16),
        scratch_shapes=[pltpu.VMEM((HEADS_PER_STEP, BAND_KEYS, BAND_Q_TILE), F32),
                        pltpu.VMEM((BAND_TILES_PER_STEP * HEADS_PER_STEP, BAND_KEYS, BAND_Q_TILE), F32),
                        pltpu.VMEM((BAND_TILES_PER_STEP * HEADS_PER_STEP, BAND_KEYS, BAND_Q_TILE), BF16)],
        compiler_params=_params("arbitrary", "arbitrary", "arbitrary"),
        name="band_prompt",
    )(q, k, vt, base)


def _rel_rows_desc(rel_table, hi, lo):
    top = jnp.broadcast_to(rel_table[-1], (max(0, hi - max(lo, REL_MAX + 1) + 1), rel_table.shape[1]))
    mid = rel_table[max(lo, -REL_MAX) + REL_MAX:min(hi, REL_MAX) + REL_MAX + 1][::-1]
    bot = jnp.broadcast_to(rel_table[0], (max(0, min(hi, -REL_MAX - 1) - lo + 1), rel_table.shape[1]))
    return jnp.concatenate([top, mid, bot], axis=0)


def _band_base(rel_table, q_rows, n_keys, width):
    offset = n_keys - q_rows
    desc = _rel_rows_desc(rel_table, offset + q_rows - 1, offset - n_keys + 1)
    pad = jnp.zeros((width - desc.shape[0], rel_table.shape[1]), rel_table.dtype)
    return jnp.concatenate([desc[q_rows - 1:], pad, desc[:q_rows - 1]], axis=0).T


def _sample_bias(rel_table, l, n_cache):
    n = n_cache + l
    desc = _rel_rows_desc(rel_table, n_cache + l - 1, -(l - 1))
    cols = jnp.stack([desc[l - 1 - i:l - 1 - i + n] for i in range(l)], axis=1)
    return jnp.swapaxes(cols, 1, 2).reshape(n, rel_table.shape[1] * l)


def _block_diag_q(q, l):
    d = q.shape[1]
    tiled = jnp.concatenate([q] * N_HEADS, axis=0)
    r = lax.broadcasted_iota(jnp.int32, (N_HEADS * l, d), 0) // l
    c = lax.broadcasted_iota(jnp.int32, (N_HEADS * l, d), 1) // HEAD_DIM
    return jnp.where(r == c, tiled, jnp.zeros_like(tiled))


def _gather_heads(r, l):
    d = r.shape[1]
    rr = lax.broadcasted_iota(jnp.int32, (N_HEADS * l, d), 0) // l
    c = lax.broadcasted_iota(jnp.int32, (N_HEADS * l, d), 1) // HEAD_DIM
    r = jnp.where(rr == c, r, 0.0)
    out = r[:l]
    for h in range(1, N_HEADS):
        out = out + r[h * l:(h + 1) * l]
    return out


def _new_key_masks(l, strict):
    j = lax.broadcasted_iota(jnp.int32, (l, N_HEADS * l), 0)
    i = lax.broadcasted_iota(jnp.int32, (l, N_HEADS * l), 1) % l
    return (j < i) if strict else (j <= i)


def _keys_values(kc_ref, vc_ref, kn_ref, vn_ref):
    def rows(c_ref):
        c = c_ref[0, 0].astype(BF16)
        return c.reshape(c.shape[0], N_HEADS * HEAD_DIM)
    return (jnp.concatenate([rows(kc_ref), kn_ref[...]], axis=0),
            jnp.concatenate([rows(vc_ref), vn_ref[...]], axis=0))


def _row_blocks(n):
    return [(r, min(r + KV_TILE, n)) for r in range(0, n, KV_TILE)]


def _prefix_sums(x):
    upto = jnp.where(lax.broadcasted_iota(jnp.int32, (KV_TILE, KV_TILE), 1)
                     <= lax.broadcasted_iota(jnp.int32, (KV_TILE, KV_TILE), 0), 1.0, 0.0).astype(BF16)
    carry = jnp.zeros((1, x.shape[1]), F32)
    out = []
    for r0, r1 in _row_blocks(x.shape[0]):
        m = r1 - r0
        c = sum(_dot(upto[:m, :m], part) for part in _split3(x[r0:r1])) + carry
        out.append(c)
        carry = c[m - 1:]
    return jnp.concatenate(out, axis=0)


def _later_sums(x):
    after = jnp.where(lax.broadcasted_iota(jnp.int32, (KV_TILE, KV_TILE), 1)
                      > lax.broadcasted_iota(jnp.int32, (KV_TILE, KV_TILE), 0), 1.0, 0.0).astype(BF16)
    carry = jnp.zeros((1, x.shape[1]), F32)
    out = []
    for r0, r1 in reversed(_row_blocks(x.shape[0])):
        m = r1 - r0
        blk = x[r0:r1]
        inner = sum(_dot(after[:m, :m], part) for part in _split2(blk))
        out.append(inner + carry)
        carry = carry + (inner[:1] + blk[:1])
    return jnp.concatenate(out[::-1], axis=0)


def _band_sample_kernel(q_ref, kn_ref, vn_ref, kc_ref, vc_ref, bias_ref, o_ref):
    l = q_ref.shape[0]
    qb = _block_diag_q(q_ref[...], l)
    kk, vv = _keys_values(kc_ref, vc_ref, kn_ref, vn_ref)
    s = _dot_nt(kk, qb) + bias_ref[...]
    p = jnp.exp(s - jnp.max(s, axis=0, keepdims=True))
    p = p / jnp.sum(p, axis=0, keepdims=True)
    o_ref[...] = _gather_heads(_dot_tn(p.astype(BF16), vv), l).astype(BF16)


def _fox_sample_kernel(q_ref, kn_ref, vn_ref, kc_ref, vc_ref, fc_ref, fn_ref, o_ref):
    l = q_ref.shape[0]
    past = kc_ref.shape[2]
    qb = _block_diag_q(q_ref[...], l)
    kk, vv = _keys_values(kc_ref, vc_ref, kn_ref, vn_ref)
    log_f = jnp.concatenate([fc_ref[0, 0], fn_ref[:, :N_HEADS]], axis=0)
    eh = lax.broadcasted_iota(jnp.int32, (N_HEADS, N_HEADS * l), 0)
    ec = lax.broadcasted_iota(jnp.int32, (N_HEADS, N_HEADS * l), 1) // l
    expand = jnp.where(eh == ec, 1.0, 0.0).astype(BF16)
    spread = sum(_dot(part, expand) for part in _split3(log_f))
    s = _dot_nt(kk, qb) - _prefix_sums(spread)
    visible = _new_key_masks(l, strict=False)
    s = jnp.concatenate([s[:past], jnp.where(visible, s[past:], NEG_INF)], axis=0)
    p = jnp.exp(s - jnp.max(s, axis=0, keepdims=True))
    p = p / jnp.sum(p, axis=0, keepdims=True)
    o_ref[...] = _gather_heads(_dot_tn(p.astype(BF16), vv), l).astype(BF16)


def _stick_sample_kernel(q_ref, kn_ref, vn_ref, kc_ref, vc_ref, o_ref):
    l = q_ref.shape[0]
    past = kc_ref.shape[2]
    qb = _block_diag_q(q_ref[...], l)
    kk, vv = _keys_values(kc_ref, vc_ref, kn_ref, vn_ref)
    z = _dot_nt(kk, qb)
    log_beta = _log_sigmoid(z)
    log_keep = log_beta - z
    earlier = _new_key_masks(l, strict=True)
    log_keep = jnp.concatenate([log_keep[:past], jnp.where(earlier, log_keep[past:], 0.0)], axis=0)
    w = jnp.exp(log_beta + _later_sums(log_keep))
    w = jnp.concatenate([w[:past], jnp.where(earlier, w[past:], 0.0)], axis=0)
    o_ref[...] = _gather_heads(_dot_tn(w.astype(BF16), vv), l).astype(BF16)


def _sample_call(kernel, name, q, k_new, v_new, cache_k, cache_v, slot, extra=(), extra_specs=()):
    _, nb, past, nh, hd = cache_k.shape
    d = nh * hd
    l = q.shape[0] // nb
    new = pl.BlockSpec((l, d), lambda b: (b, 0))
    cache = pl.BlockSpec((1, 1, past, nh, hd), lambda b: (slot, b, 0, 0, 0))
    return pl.pallas_call(
        kernel,
        grid=(nb,),
        in_specs=[new, new, new, cache, cache, *extra_specs],
        out_specs=new,
        out_shape=jax.ShapeDtypeStruct(q.shape, BF16),
        compiler_params=_params("parallel"),
        name=name,
    )(q, k_new, v_new, cache_k, cache_v, *extra)


def _cumsum_kernel(f_ref, o_ref):
    upto = jnp.where(lax.broadcasted_iota(jnp.int32, (KV_TILE, KV_TILE), 1)
                     <= lax.broadcasted_iota(jnp.int32, (KV_TILE, KV_TILE), 0), 1.0, 0.0).astype(BF16)
    carry = jnp.zeros((1, f_ref.shape[1]), F32)
    for t in range(f_ref.shape[0] // KV_TILE):
        rows = slice(t * KV_TILE, (t + 1) * KV_TILE)
        c = sum(_dot(upto, part) for part in _split3(f_ref[rows, :])) + carry
        o_ref[rows, :] = c
        carry = c[KV_TILE - 1:, :]


def _cumsum_rows(log_f, s):
    spec = pl.BlockSpec((s, log_f.shape[1]), lambda i: (i, 0))
    return pl.pallas_call(
        _cumsum_kernel, grid=(log_f.shape[0] // s,), in_specs=[spec], out_specs=spec,
        out_shape=jax.ShapeDtypeStruct(log_f.shape, F32),
        compiler_params=_params("parallel"), name="forget_cumsum",
    )(log_f)


def kernel(x_prompt, x_sample, cache_a_k, cache_a_v, cache_b_k, cache_b_v, cache_b_logf, cache_c_k, cache_c_v, norm_ffn1, ffn1_gate, ffn1_up, ffn1_down, norm_mix, norm_ffn2, ffn2_gate, ffn2_up, ffn2_down, a_w_qkv, a_w_o, a_rel_bias, b_w_qkv, b_w_o, b_w_f, b_b_f, c_w_qkv, c_w_o, norm_final):
    b, s, d = x_prompt.shape
    nb, l, _ = x_sample.shape
    depth = norm_ffn1.shape[0]
    assert d == N_HEADS * HEAD_DIM and s % CAUSAL_TILE == 0 and s >= A_WINDOW and A_WINDOW % BAND_Q_TILE == 0

    xp = x_prompt.reshape(b * s, d)
    xs = x_sample.reshape(nb * l, d)
    heads = (N_HEADS, HEAD_DIM)
    order = ("a_kp", "a_vp", "a_ks", "a_vs", "b_kp", "b_vp", "b_fp", "b_ks", "b_vs", "b_fs",
             "c_kp", "c_vp", "c_ks", "c_vs")
    outs = {name: [] for name in order}
    w1 = (ffn1_gate.astype(BF16), ffn1_up.astype(BF16), ffn1_down.astype(BF16))
    w2 = (ffn2_gate.astype(BF16), ffn2_up.astype(BF16), ffn2_down.astype(BF16))
    w_qkv = (a_w_qkv.astype(BF16), b_w_qkv.astype(BF16), c_w_qkv.astype(BF16))
    w_out = (a_w_o.astype(BF16), b_w_o.astype(BF16), c_w_o.astype(BF16))
    forget_w = (jnp.pad(b_w_f, ((0, 0), (0, 0), (0, LANE - N_HEADS))).astype(BF16),
                jnp.pad(b_b_f, ((0, 0), (0, LANE - N_HEADS))).reshape(-1, 1, LANE))
    keep = min(A_WINDOW, s)
    assert keep == ROW_TILE
    for i in range(depth):
        kind, slot = i % 3, i // 3
        xp = _ffn(xp, i, norm_ffn1, *w1)
        xs = _ffn(xs, i, norm_ffn1, *w1)
        forget = forget_w if kind == 1 else None
        qp, kp, vp, kpb, vpt, *rest_p = _qkv(xp, i, slot, norm_mix, w_qkv[kind], forget, transposed_v=True,
                                             keep_last=s if kind == 0 else None)
        qs, ks, vs, ksb, vsb, *rest_s = _qkv(xs, i, slot, norm_mix, w_qkv[kind], forget)
        if kind == 0:
            base = _band_base(a_rel_bias[slot], BAND_Q_TILE, BAND_KEYS, BIAS_WIDTH)
            base = base.reshape(N_HEAD_PAIRS, HEADS_PER_STEP, BIAS_WIDTH)
            op = _band_prompt(qp, kpb, vpt, base, b, s)
            bias_s = _sample_bias(a_rel_bias[slot], l, cache_a_k.shape[2])
            os_ = _sample_call(_band_sample_kernel, "band_sample", qs, ksb, vsb, cache_a_k, cache_a_v, slot,
                               extra=(bias_s,), extra_specs=(_resident(bias_s.shape),))
            outs["a_kp"].append(rest_p[0])
            outs["a_vp"].append(rest_p[1])
            outs["a_ks"].append(ks.reshape(nb, l, *heads))
            outs["a_vs"].append(vs.reshape(nb, l, *heads))
        elif kind == 1:
            fp, fs = rest_p[0], rest_s[0]
            op = _fox_prompt(qp, kpb, vpt, _cumsum_rows(fp, s), b, s)
            past = cache_b_k.shape[2]
            os_ = _sample_call(_fox_sample_kernel, "fox_sample", qs, ksb, vsb, cache_b_k, cache_b_v, slot,
                               extra=(cache_b_logf, fs),
                               extra_specs=(pl.BlockSpec((1, 1, past, N_HEADS), lambda bi: (slot, bi, 0, 0)),
                                            pl.BlockSpec((l, LANE), lambda bi: (bi, 0))))
            outs["b_kp"].append(kp.reshape(b, s, *heads))
            outs["b_vp"].append(vp.reshape(b, s, *heads))
            outs["b_fp"].append(fp[:, :N_HEADS].reshape(b, s, N_HEADS))
            outs["b_ks"].append(ks.reshape(nb, l, *heads))
            outs["b_vs"].append(vs.reshape(nb, l, *heads))
            outs["b_fs"].append(fs[:, :N_HEADS].reshape(nb, l, N_HEADS))
        else:
            op = _stick_prompt(qp, kpb, vpt, b, s)
            os_ = _sample_call(_stick_sample_kernel, "stick_sample", qs, ksb, vsb, cache_c_k, cache_c_v, slot)
            outs["c_kp"].append(kp.reshape(b, s, *heads))
            outs["c_vp"].append(vp.reshape(b, s, *heads))
            outs["c_ks"].append(ks.reshape(nb, l, *heads))
            outs["c_vs"].append(vs.reshape(nb, l, *heads))
        final = norm_final if i == depth - 1 else None
        xp = _ffn(xp, i, norm_ffn2, *w2, proj=(op, w_out[kind], slot), final_g=final)
        xs = _ffn(xs, i, norm_ffn2, *w2, proj=(os_, w_out[kind], slot), final_g=final)

    return (xp.reshape(b, s, d), xs.reshape(nb, l, d)) + tuple(jnp.stack(outs[name]) for name in order)
```

```python
import functools

import jax
import jax.numpy as jnp
from jax import lax
from jax.experimental import pallas as pl
from jax.experimental.pallas import tpu as pltpu

F32 = jnp.float32
BF16 = jnp.bfloat16

N_HEADS = 16
HEAD_DIM = 64
CHUNK = 64
LEFT_CHUNKS = 8
A_WINDOW = LEFT_CHUNKS * CHUNK
REL_MAX = 256
RMS_EPS = 1e-6
ATTN_SCALE = HEAD_DIM ** -0.5
NEG_INF = -1e30

LANE = 128
HEADS_PER_STEP = LANE // HEAD_DIM
N_HEAD_PAIRS = N_HEADS // HEADS_PER_STEP
VMEM_LIMIT = 56 * 1024 * 1024

ROW_TILE = 512
FF_TILE = 256
KV_TILE = 256
CAUSAL_TILE = 512
BAND_Q_TILE = 256
BAND_TILES_PER_STEP = 4
BAND_KEYS = BAND_Q_TILE + A_WINDOW
BIAS_WIDTH = 1024


def _params(*sem):
    return pltpu.CompilerParams(dimension_semantics=sem, vmem_limit_bytes=VMEM_LIMIT)


def _rms(x, g):
    y = x * lax.rsqrt(jnp.mean(x * x, axis=-1, keepdims=True) + RMS_EPS)
    return y * g


def _log_sigmoid(z):
    return jnp.minimum(z, 0.0) - jnp.log(1.0 + jnp.exp(-jnp.abs(z)))


def _dot(a, b):
    return jnp.dot(a, b, preferred_element_type=F32)


def _dot_nt(a, b):
    return lax.dot_general(a, b, (((1,), (1,)), ((), ())), preferred_element_type=F32)


def _split2(x):
    hi = x.astype(BF16)
    lo = (x - hi.astype(F32)).astype(BF16)
    return hi, lo


def _split3(x):
    hi, lo = _split2(x)
    lo2 = (x - hi.astype(F32) - lo.astype(F32)).astype(BF16)
    return hi, lo, lo2


def _resident(shape):
    return pl.BlockSpec(shape, lambda *_: (0,) * len(shape), pipeline_mode=pl.Buffered(1))


def _resident_layer(shape, layer):
    return pl.BlockSpec((None,) + tuple(shape[1:]), lambda *_: (layer,) + (0,) * (len(shape) - 1),
                        pipeline_mode=pl.Buffered(1))


def _ffn_kernel(*refs, has_proj, has_final):
    refs = list(refs)
    x_ref = refs.pop(0)
    if has_proj:
        o_ref, wo_ref = refs.pop(0), refs.pop(0)
    g_ref, wg_ref, wu_ref, wd_ref = refs[:4]
    refs = refs[4:]
    if has_final:
        gf_ref = refs.pop(0)
    out_ref = refs.pop(0)

    x = x_ref[...]
    if has_proj:
        x = x + _dot(o_ref[...], wo_ref[...])
    h = _rms(x, g_ref[...]).astype(BF16)
    d_ff = wg_ref.shape[1]
    acc = jnp.zeros(x.shape, F32)
    for c in range(d_ff // FF_TILE):
        sl = slice(c * FF_TILE, (c + 1) * FF_TILE)
        gate = _dot(h, wg_ref[:, sl])
        up = _dot(h, wu_ref[:, sl])
        act = (gate * jax.nn.sigmoid(gate) * up).astype(BF16)
        acc = acc + _dot(act, wd_ref[sl, :])
    y = x + 0.5 * acc
    if has_final:
        y = _rms(y, gf_ref[...])
    out_ref[...] = y


def _ffn(x, layer, g, wg, wu, wd, proj=None, final_g=None):
    n, d = x.shape
    d_ff = wg.shape[2]
    tm = min(ROW_TILE, n)
    assert n % tm == 0 and d_ff % FF_TILE == 0
    row = pl.BlockSpec((tm, d), lambda i: (i, 0))
    args, specs = [x], [row]
    if proj is not None:
        o, w_o, slot = proj
        args += [o, w_o]
        specs += [row, _resident_layer(w_o.shape, slot)]
    g = g.reshape(g.shape[0], 1, d)
    args += [g, wg, wu, wd]
    specs += [_resident_layer(a.shape, layer) for a in (g, wg, wu, wd)]
    if final_g is not None:
        args.append(final_g.reshape(1, d))
        specs.append(_resident((1, d)))
    return pl.pallas_call(
        functools.partial(_ffn_kernel, has_proj=proj is not None, has_final=final_g is not None),
        grid=(n // tm,),
        in_specs=specs,
        out_specs=row,
        out_shape=jax.ShapeDtypeStruct((n, d), F32),
        compiler_params=_params("parallel"),
        name="ffn",
    )(*args)


def _qkv_kernel(*refs, has_forget, transposed, has_keep):
    x_ref, g_ref, w_ref = refs[:3]
    refs = list(refs[3:])
    if has_forget:
        wf_ref, bf_ref = refs[:2]
        refs = refs[2:]
    q_ref, k_ref, v_ref, kb_ref, vb_ref = refs[:5]
    refs = refs[5:]
    d = x_ref.shape[1]
    h = _rms(x_ref[...], g_ref[...]).astype(BF16)
    q_ref[...] = (_dot(h, w_ref[:, :d]) * ATTN_SCALE).astype(BF16)
    k = _dot(h, w_ref[:, d:2 * d])
    v = _dot(h, w_ref[:, 2 * d:])
    kb_ref[...] = k.astype(BF16)
    if transposed:
        kt, vt = k.T, v.T
        k_ref[0] = kt
        v_ref[0] = vt
        vtb = vt.astype(BF16)
        for t in range(vb_ref.shape[0]):
            vb_ref[t] = vtb[:, t * KV_TILE:(t + 1) * KV_TILE]
    else:
        k_ref[...] = k.reshape(k_ref.shape)
        v_ref[...] = v.reshape(v_ref.shape)
        vb_ref[...] = v.astype(BF16)
    if has_forget:
        refs.pop(0)[...] = _log_sigmoid(_dot(h, wf_ref[...]) + bf_ref[...])
    if has_keep:
        refs[0][...] = kt
        refs[1][...] = vt


def _qkv(x, layer, slot, g, w, forget=None, seq=None, keep_last=False):
    n, d = x.shape
    tm = min(ROW_TILE, n)
    assert n % tm == 0 and tm % KV_TILE == 0
    row = pl.BlockSpec((tm, d), lambda i: (i, 0))
    g = g.reshape(g.shape[0], 1, d)
    args = [x, g, w]
    specs = [row, _resident_layer(g.shape, layer), _resident_layer(w.shape, slot)]
    if seq is not None:
        assert seq % tm == 0 and n % seq == 0
        per = seq // tm
        kv_shape = jax.ShapeDtypeStruct((n // seq, d, seq), F32)
        kv_spec = pl.BlockSpec((1, d, tm), lambda i: (i // per, 0, i % per))
        vb_shape = jax.ShapeDtypeStruct((n // KV_TILE, d, KV_TILE), BF16)
        vb_spec = pl.BlockSpec((tm // KV_TILE, d, KV_TILE), lambda i: (i, 0, 0))
    else:
        kv_shape = jax.ShapeDtypeStruct((n, N_HEADS, HEAD_DIM), F32)
        kv_spec = pl.BlockSpec((tm, N_HEADS, HEAD_DIM), lambda i: (i, 0, 0))
        vb_shape, vb_spec = jax.ShapeDtypeStruct((n, d), BF16), row
    shapes = [jax.ShapeDtypeStruct((n, d), BF16), kv_shape, kv_shape, jax.ShapeDtypeStruct((n, d), BF16), vb_shape]
    out_specs = [row, kv_spec, kv_spec, row, vb_spec]
    if forget is not None:
        args += list(forget)
        specs += [_resident_layer(a.shape, slot) for a in forget]
        shapes.append(jax.ShapeDtypeStruct((n, LANE), F32))
        out_specs.append(pl.BlockSpec((tm, LANE), lambda i: (i, 0)))
    if keep_last:
        keep_spec = pl.BlockSpec((None, d, tm), lambda i: (i // per, 0, 0))
        shapes += [jax.ShapeDtypeStruct((n // seq, d, tm), F32)] * 2
        out_specs += [keep_spec, keep_spec]
    return pl.pallas_call(
        functools.partial(_qkv_kernel, has_forget=forget is not None, transposed=seq is not None,
                          has_keep=keep_last),
        grid=(n // tm,),
        in_specs=specs,
        out_specs=out_specs,
        out_shape=shapes,
        compiler_params=_params("arbitrary" if keep_last else "parallel"),
        name="qkv",
    )(*args)


def _pair_operands(q):
    lane_head = lax.broadcasted_iota(jnp.int32, q.shape, 1) // HEAD_DIM
    return [jnp.where(lane_head == j, q, jnp.zeros_like(q)) for j in range(HEADS_PER_STEP)]


def _vt_tile(vt_ref, t, j, with_ones):
    vt = vt_ref[t]
    if not with_ones:
        return vt
    row_head = lax.broadcasted_iota(jnp.int32, vt.shape, 0) // HEAD_DIM
    return jnp.where(row_head == j, vt, jnp.ones_like(vt))


def _pv_t(vt_ref, first_tile, p, j, with_ones):
    out = None
    for t in range(p.shape[0] // KV_TILE):
        part = _dot(_vt_tile(vt_ref, first_tile + t, j, with_ones), p[t * KV_TILE:(t + 1) * KV_TILE])
        out = part if out is None else out + part
    return out


def _merge_heads_t(per_head, normalise):
    parts = []
    for j, acc in enumerate(per_head):
        own = acc[j * HEAD_DIM:(j + 1) * HEAD_DIM]
        if normalise:
            other = (1 - j) * HEAD_DIM
            own = own / acc[other:other + 1]
        parts.append(own)
    return jnp.concatenate(parts, axis=0).T.astype(BF16)


def _causal_specs(s):
    seq = pl.BlockSpec((s, LANE), lambda b, hp: (b, hp))
    return seq, pl.BlockSpec((s // KV_TILE, LANE, KV_TILE), lambda b, hp: (b, hp, 0))


def _fox_prompt_kernel(q_ref, k_ref, vt_ref, cum_ref, o_ref, cumrep_ref, m_ref, acc_ref, s_ref, p_ref):
    hp = pl.program_id(1)
    tq = CAUSAL_TILE
    reps = tq // LANE
    n_kv = tq // KV_TILE
    c = cum_ref[...]
    lane = lax.broadcasted_iota(jnp.int32, c.shape, 1)
    for j in range(HEADS_PER_STEP):
        col = jnp.sum(jnp.where(lane == hp * HEADS_PER_STEP + j, c, 0.0), axis=1, keepdims=True)
        cumrep_ref[j] = jnp.broadcast_to(col, c.shape)
    visible = (lax.broadcasted_iota(jnp.int32, (tq, tq), 0) <= lax.broadcasted_iota(jnp.int32, (tq, tq), 1))
    pair_no = 0
    for qi in range(q_ref.shape[0] // tq):
        q_rows = slice(qi * tq, (qi + 1) * tq)
        q_pad = _pair_operands(q_ref[q_rows, :])
        m_ref[...] = jnp.full(m_ref.shape, NEG_INF, F32)
        acc_ref[...] = jnp.zeros(acc_ref.shape, F32)
        for kb in range(qi + 1):
            slot = pair_no % 2
            pair_no += 1
            k_rows = slice(kb * tq, (kb + 1) * tq)
            k = k_ref[k_rows, :]
            stats = []
            for j in range(HEADS_PER_STEP):
                s = _dot_nt(k, q_pad[j])
                s = s - jnp.concatenate([cumrep_ref[j, k_rows, :]] * reps, axis=1)
                if kb == qi:
                    s = jnp.where(visible, s, NEG_INF)
                s_ref[slot, j] = s
                m_old = m_ref[j]
                m_new = jnp.maximum(m_old, jnp.max(s, axis=0, keepdims=True))
                m_ref[j] = m_new
                stats.append((m_new, jnp.exp(m_old - m_new)))
            for j in range(HEADS_PER_STEP):
                p_ref[slot, j] = jnp.exp(s_ref[slot, j] - stats[j][0]).astype(BF16)
            for j in range(HEADS_PER_STEP):
                acc_ref[j] = stats[j][1] * acc_ref[j] + _pv_t(vt_ref, kb * n_kv, p_ref[slot, j], j, True)
        o_ref[q_rows, :] = _merge_heads_t([acc_ref[j] for j in range(HEADS_PER_STEP)], True)


def _fox_prompt(q, k, vt, cum, b, s):
    tq = CAUSAL_TILE
    seq, vt_spec = _causal_specs(s)
    cum_spec = pl.BlockSpec((s, LANE), lambda bi, hp: (bi, 0))
    return pl.pallas_call(
        _fox_prompt_kernel,
        grid=(b, N_HEAD_PAIRS),
        in_specs=[seq, seq, vt_spec, cum_spec],
        out_specs=seq,
        out_shape=jax.ShapeDtypeStruct(q.shape, BF16),
        scratch_shapes=[pltpu.VMEM((HEADS_PER_STEP, s, LANE), F32),
                        pltpu.VMEM((HEADS_PER_STEP, 1, tq), F32),
                        pltpu.VMEM((HEADS_PER_STEP, LANE, tq), F32),
                        pltpu.VMEM((2, HEADS_PER_STEP, tq, tq), F32),
                        pltpu.VMEM((2, HEADS_PER_STEP, tq, tq), BF16)],
        compiler_params=_params("parallel", "parallel"),
        name="fox_prompt",
    )(q, k, vt, cum)


def _stick_prompt_kernel(q_ref, k_ref, vt_ref, o_ref, later_ref, acc_ref, z_ref, lb_ref, x_ref, first_ref,
                         tail_ref, w_ref):
    tq = CAUSAL_TILE
    n_kv = tq // KV_TILE
    earlier = lax.broadcasted_iota(jnp.int32, (tq, tq), 0) < lax.broadcasted_iota(jnp.int32, (tq, tq), 1)
    after = jnp.where(lax.broadcasted_iota(jnp.int32, (KV_TILE, KV_TILE), 1)
                      > lax.broadcasted_iota(jnp.int32, (KV_TILE, KV_TILE), 0), 1.0, 0.0).astype(BF16)
    after2 = jnp.concatenate([after, after], axis=1)
    pair_no = 0
    for qi in range(q_ref.shape[0] // tq):
        q_rows = slice(qi * tq, (qi + 1) * tq)
        q_pad = _pair_operands(q_ref[q_rows, :])
        later_ref[...] = jnp.zeros(later_ref.shape, F32)
        acc_ref[...] = jnp.zeros(acc_ref.shape, F32)
        for kb in reversed(range(qi + 1)):
            slot = pair_no % 2
            pair_no += 1
            diagonal = kb == qi
            k = k_ref[kb * tq:(kb + 1) * tq, :]
            for j in range(HEADS_PER_STEP):
                z_ref[slot, j] = _dot_nt(k, q_pad[j])
            for j in range(HEADS_PER_STEP):
                z = z_ref[slot, j]
                log_beta = _log_sigmoid(z)
                log_keep = log_beta - z
                if diagonal:
                    log_keep = jnp.where(earlier, log_keep, 0.0)
                lb_ref[slot, j] = log_beta
                for t in range(n_kv):
                    hi, lo = _split2(log_keep[t * KV_TILE:(t + 1) * KV_TILE])
                    x_ref[slot, j, t, :KV_TILE, :] = hi
                    x_ref[slot, j, t, KV_TILE:, :] = lo
                    first_ref[slot, j, t] = log_keep[t * KV_TILE:t * KV_TILE + 1]
            for j in range(HEADS_PER_STEP):
                later = later_ref[j]
                for t in reversed(range(n_kv)):
                    inner = _dot(after2, x_ref[slot, j, t])
                    tail_ref[slot, j, t * KV_TILE:(t + 1) * KV_TILE, :] = inner + later
                    later = later + (inner[:1] + first_ref[slot, j, t])
                later_ref[j] = later
            for j in range(HEADS_PER_STEP):
                w = jnp.exp(lb_ref[slot, j] + tail_ref[slot, j])
                if diagonal:
                    w = jnp.where(earlier, w, 0.0)
                w_ref[slot, j] = w.astype(BF16)
            for j in range(HEADS_PER_STEP):
                acc_ref[j] = acc_ref[j] + _pv_t(vt_ref, kb * n_kv, w_ref[slot, j], j, False)
        o_ref[q_rows, :] = _merge_heads_t([acc_ref[j] for j in range(HEADS_PER_STEP)], False)


def _stick_prompt(q, k, vt, b, s):
    tq = CAUSAL_TILE
    seq, vt_spec = _causal_specs(s)
    stage = (2, HEADS_PER_STEP)
    return pl.pallas_call(
        _stick_prompt_kernel,
        grid=(b, N_HEAD_PAIRS),
        in_specs=[seq, seq, vt_spec],
        out_specs=seq,
        out_shape=jax.ShapeDtypeStruct(q.shape, BF16),
        scratch_shapes=[pltpu.VMEM((HEADS_PER_STEP, 1, tq), F32),
                        pltpu.VMEM((HEADS_PER_STEP, LANE, tq), F32),
                        pltpu.VMEM(stage + (tq, tq), F32),
                        pltpu.VMEM(stage + (tq, tq), F32),
                        pltpu.VMEM(stage + (tq // KV_TILE, 2 * KV_TILE, tq), BF16),
                        pltpu.VMEM(stage + (tq // KV_TILE, 1, tq), F32),
                        pltpu.VMEM(stage + (tq, tq), F32),
                        pltpu.VMEM(stage + (tq, tq), BF16)],
        compiler_params=_params("parallel", "parallel"),
        name="stick_prompt",
    )(q, k, vt)


def _band_prompt_kernel(q_ref, k_ref, vt_ref, base_ref, o_ref, bias_ref, s_ref, p_ref):
    b, qg = pl.program_id(1), pl.program_id(2)
    tq = BAND_Q_TILE
    n_sub = q_ref.shape[0] // tq

    @pl.when((b == 0) & (qg == 0))
    def _():
        i = lax.broadcasted_iota(jnp.int32, (tq, BAND_KEYS), 0) // CHUNK
        m = lax.broadcasted_iota(jnp.int32, (tq, BAND_KEYS), 1) // CHUNK
        in_band = (m >= i) & (m <= i + LEFT_CHUNKS)
        for j in range(HEADS_PER_STEP):
            rows = jnp.broadcast_to(base_ref[0, j:j + 1, :], (tq, BIAS_WIDTH))
            rolled = pltpu.roll(rows, 0, 1, stride=1, stride_axis=0)
            bias_ref[j] = jnp.where(in_band, rolled[:, :BAND_KEYS], NEG_INF).T

    def attend(tiles):
        maxima = []
        for sub, (first_tile, n_tiles) in enumerate(tiles):
            q_pad = _pair_operands(q_ref[sub * tq:(sub + 1) * tq, :])
            width = n_tiles * KV_TILE
            start = first_tile * KV_TILE
            if not isinstance(start, int):
                start = pl.multiple_of(start, KV_TILE)
            k = k_ref[pl.ds(start, width), :]
            for j in range(HEADS_PER_STEP):
                s = _dot_nt(k, q_pad[j]) + bias_ref[j, BAND_KEYS - width:, :]
                s_ref[sub * HEADS_PER_STEP + j, :width, :] = s
                maxima.append(jnp.max(s, axis=0, keepdims=True))
        for sub, (first_tile, n_tiles) in enumerate(tiles):
            width = n_tiles * KV_TILE
            for j in range(HEADS_PER_STEP):
                c = sub * HEADS_PER_STEP + j
                p_ref[c, :width, :] = jnp.exp(s_ref[c, :width, :] - maxima[c]).astype(BF16)
        for sub, (first_tile, n_tiles) in enumerate(tiles):
            width = n_tiles * KV_TILE
            per_head = [_pv_t(vt_ref, first_tile, p_ref[sub * HEADS_PER_STEP + j, :width, :], j, True)
                        for j in range(HEADS_PER_STEP)]
            o_ref[sub * tq:(sub + 1) * tq, :] = _merge_heads_t(per_head, True)

    n_lead = A_WINDOW // tq
    band_tiles = BAND_KEYS // KV_TILE
    n_lead_steps = -(-n_lead // n_sub)
    for g in range(n_lead_steps):
        lead = [(0, t + 1) if t < n_lead else (t - n_lead, band_tiles) for t in range(g * n_sub, (g + 1) * n_sub)]
        pl.when(qg == g)(functools.partial(attend, lead))

    @pl.when(qg >= n_lead_steps)
    def _():
        attend([(qg * n_sub + sub - n_lead, band_tiles) for sub in range(n_sub)])


def _band_prompt(q, k, vt, base, b, s):
    tq = BAND_Q_TILE * BAND_TILES_PER_STEP
    nq = s // tq
    q_spec = pl.BlockSpec((tq, LANE), lambda hp, bi, qi: (bi * nq + qi, hp))
    k_spec = pl.BlockSpec((s, LANE), lambda hp, bi, qi: (bi, hp))
    vt_spec = pl.BlockSpec((s // KV_TILE, LANE, KV_TILE), lambda hp, bi, qi: (bi, hp, 0))
    base_spec = pl.BlockSpec((1, HEADS_PER_STEP, BIAS_WIDTH), lambda hp, bi, qi: (hp, 0, 0))
    return pl.pallas_call(
        _band_prompt_kernel,
        grid=(N_HEAD_PAIRS, b, s // tq),
        in_specs=[q_spec, k_spec, vt_spec, base_spec],
        out_specs=q_spec,
        out_shape=jax.ShapeDtypeStruct(q.shape, BF16),
        scratch_shapes=[pltpu.VMEM((HEADS_PER_STEP, BAND_KEYS, BAND_Q_TILE), F32),
                        pltpu.VMEM((BAND_TILES_PER_STEP * HEADS_PER_STEP, BAND_KEYS, BAND_Q_TILE), F32),
                        pltpu.VMEM((BAND_TILES_PER_STEP * HEADS_PER_STEP, BAND_KEYS, BAND_Q_TILE), BF16)],
        compiler_params=_params("arbitrary", "arbitrary", "arbitrary"),
        name="band_prompt",
    )(q, k, vt, base)


def _rel_rows_desc(rel_table, hi, lo):
    top = jnp.broadcast_to(rel_table[-1], (max(0, hi - max(lo, REL_MAX + 1) + 1), rel_table.shape[1]))
    mid = rel_table[max(lo, -REL_MAX) + REL_MAX:min(hi, REL_MAX) + REL_MAX + 1][::-1]
    bot = jnp.broadcast_to(rel_table[0], (max(0, min(hi, -REL_MAX - 1) - lo + 1), rel_table.shape[1]))
    return jnp.concatenate([top, mid, bot], axis=0)


def _band_base(rel_table, q_rows, n_keys, width):
    offset = n_keys - q_rows
    desc = _rel_rows_desc(rel_table, offset + q_rows - 1, offset - n_keys + 1)
    pad = jnp.zeros((width - desc.shape[0], rel_table.shape[1]), rel_table.dtype)
    return jnp.concatenate([desc[q_rows - 1:], pad, desc[:q_rows - 1]], axis=0).T


def _sample_bias(rel_table, l, n_cache):
    n = n_cache + l
    desc = _rel_rows_desc(rel_table, n_cache + l - 1, -(l - 1))
    cols = jnp.stack([desc[l - 1 - i:l - 1 - i + n] for i in range(l)], axis=1)
    return jnp.swapaxes(cols, 1, 2).reshape(n, rel_table.shape[1] * l)


def _block_diag_q(q, l):
    d = q.shape[1]
    tiled = jnp.concatenate([q] * N_HEADS, axis=0)
    r = lax.broadcasted_iota(jnp.int32, (N_HEADS * l, d), 0) // l
    c = lax.broadcasted_iota(jnp.int32, (N_HEADS * l, d), 1) // HEAD_DIM
    return jnp.where(r == c, tiled, jnp.zeros_like(tiled))


def _gather_heads(r, l):
    d = r.shape[1]
    rr = lax.broadcasted_iota(jnp.int32, (N_HEADS * l, d), 0) // l
    c = lax.broadcasted_iota(jnp.int32, (N_HEADS * l, d), 1) // HEAD_DIM
    r = jnp.where(rr == c, r, 0.0)
    out = r[:l]
    for h in range(1, N_HEADS):
        out = out + r[h * l:(h + 1) * l]
    return out


def _new_key_masks(l, strict):
    i = lax.broadcasted_iota(jnp.int32, (N_HEADS * l, l), 0) % l
    j = lax.broadcasted_iota(jnp.int32, (N_HEADS * l, l), 1)
    return (j < i) if strict else (j <= i)


def _sample_scores(q_ref, kn_ref, kt_ref):
    l = q_ref.shape[0]
    qb = _block_diag_q(q_ref[...], l)
    return _dot(qb, kt_ref[0, 0].astype(BF16)), _dot_nt(qb, kn_ref[...]), l


def _sample_output(w_cache, w_new, vt_ref, vn_ref, l):
    r = _dot_nt(w_cache.astype(BF16), vt_ref[0, 0].astype(BF16)) + _dot(w_new.astype(BF16), vn_ref[...])
    return _gather_heads(r, l).astype(BF16)


def _softmax_two(s_cache, s_new):
    m = jnp.maximum(jnp.max(s_cache, axis=1, keepdims=True), jnp.max(s_new, axis=1, keepdims=True))
    p_cache, p_new = jnp.exp(s_cache - m), jnp.exp(s_new - m)
    inv = 1.0 / (jnp.sum(p_cache, axis=1, keepdims=True) + jnp.sum(p_new, axis=1, keepdims=True))
    return p_cache * inv, p_new * inv


def _col_blocks(x):
    n = x.shape[1]
    return [x[:, c:min(c + KV_TILE, n)] for c in range(0, n, KV_TILE)]


def _prefix_sums_cols(blocks):
    upto = jnp.where(lax.broadcasted_iota(jnp.int32, (KV_TILE, KV_TILE), 0)
                     <= lax.broadcasted_iota(jnp.int32, (KV_TILE, KV_TILE), 1), 1.0, 0.0).astype(BF16)
    carry = jnp.zeros((blocks[0].shape[0], 1), F32)
    out = []
    for x in blocks:
        m = x.shape[1]
        c = sum(_dot(part, upto[:m, :m]) for part in _split3(x)) + carry
        out.append(c)
        carry = c[:, m - 1:]
    return out


def _later_sums_cols(blocks):
    after = jnp.where(lax.broadcasted_iota(jnp.int32, (KV_TILE, KV_TILE), 0)
                      > lax.broadcasted_iota(jnp.int32, (KV_TILE, KV_TILE), 1), 1.0, 0.0).astype(BF16)
    carry = jnp.zeros((blocks[0].shape[0], 1), F32)
    out = []
    for x in reversed(blocks):
        m = x.shape[1]
        inner = sum(_dot(part, after[:m, :m]) for part in _split2(x))
        out.append(inner + carry)
        carry = carry + (inner[:, :1] + x[:, :1])
    return out[::-1]


def _band_sample_kernel(q_ref, kn_ref, vn_ref, kt_ref, vt_ref, bias_c_ref, bias_n_ref, o_ref):
    s_cache, s_new, l = _sample_scores(q_ref, kn_ref, kt_ref)
    p_cache, p_new = _softmax_two(s_cache + bias_c_ref[...], s_new + bias_n_ref[...])
    o_ref[...] = _sample_output(p_cache, p_new, vt_ref, vn_ref, l)


def _fox_sample_kernel(q_ref, kn_ref, vn_ref, kt_ref, vt_ref, fc_ref, fn_ref, o_ref):
    s_cache, s_new, l = _sample_scores(q_ref, kn_ref, kt_ref)
    eh = lax.broadcasted_iota(jnp.int32, (N_HEADS * l, N_HEADS), 0) // l
    ec = lax.broadcasted_iota(jnp.int32, (N_HEADS * l, N_HEADS), 1)
    expand = jnp.where(eh == ec, 1.0, 0.0).astype(BF16)
    spread_c = sum(_dot(expand, part) for part in _split3(fc_ref[0, 0]))
    spread_n = sum(_dot_nt(expand, part) for part in _split3(fn_ref[:, :N_HEADS]))
    cums = _prefix_sums_cols(_col_blocks(spread_c) + [spread_n])
    s_cache = s_cache - jnp.concatenate(cums[:-1], axis=1)
    s_new = jnp.where(_new_key_masks(l, strict=False), s_new - cums[-1], NEG_INF)
    p_cache, p_new = _softmax_two(s_cache, s_new)
    o_ref[...] = _sample_output(p_cache, p_new, vt_ref, vn_ref, l)


def _stick_sample_kernel(q_ref, kn_ref, vn_ref, kt_ref, vt_ref, o_ref):
    z_cache, z_new, l = _sample_scores(q_ref, kn_ref, kt_ref)
    earlier = _new_key_masks(l, strict=True)
    lb_cache, lb_new = _log_sigmoid(z_cache), _log_sigmoid(z_new)
    keep_new = jnp.where(earlier, lb_new - z_new, 0.0)
    tails = _later_sums_cols(_col_blocks(lb_cache - z_cache) + [keep_new])
    w_cache = jnp.exp(lb_cache + jnp.concatenate(tails[:-1], axis=1))
    w_new = jnp.where(earlier, jnp.exp(lb_new + tails[-1]), 0.0)
    o_ref[...] = _sample_output(w_cache, w_new, vt_ref, vn_ref, l)


def _cache_t(cache):
    n, nb, past, nh, hd = cache.shape
    return jnp.transpose(cache, (0, 1, 3, 4, 2)).reshape(n, nb, nh * hd, past)


def _sample_call(kernel, name, q, k_new, v_new, cache_kt, cache_vt, slot, extra=(), extra_specs=()):
    _, nb, d, past = cache_kt.shape
    l = q.shape[0] // nb
    new = pl.BlockSpec((l, d), lambda b: (b, 0))
    cache = pl.BlockSpec((1, 1, d, past), lambda b: (slot, b, 0, 0))
    return pl.pallas_call(
        kernel,
        grid=(nb,),
        in_specs=[new, new, new, cache, cache, *extra_specs],
        out_specs=new,
        out_shape=jax.ShapeDtypeStruct(q.shape, BF16),
        compiler_params=_params("parallel"),
        name=name,
    )(q, k_new, v_new, cache_kt, cache_vt, *extra)


def _cumsum_kernel(f_ref, o_ref):
    upto = jnp.where(lax.broadcasted_iota(jnp.int32, (KV_TILE, KV_TILE), 1)
                     <= lax.broadcasted_iota(jnp.int32, (KV_TILE, KV_TILE), 0), 1.0, 0.0).astype(BF16)
    carry = jnp.zeros((1, f_ref.shape[1]), F32)
    for t in range(f_ref.shape[0] // KV_TILE):
        rows = slice(t * KV_TILE, (t + 1) * KV_TILE)
        c = sum(_dot(upto, part) for part in _split3(f_ref[rows, :])) + carry
        o_ref[rows, :] = c
        carry = c[KV_TILE - 1:, :]


def _cumsum_rows(log_f, s):
    spec = pl.BlockSpec((s, log_f.shape[1]), lambda i: (i, 0))
    return pl.pallas_call(
        _cumsum_kernel, grid=(log_f.shape[0] // s,), in_specs=[spec], out_specs=spec,
        out_shape=jax.ShapeDtypeStruct(log_f.shape, F32),
        compiler_params=_params("parallel"), name="forget_cumsum",
    )(log_f)


def kernel(x_prompt, x_sample, cache_a_k, cache_a_v, cache_b_k, cache_b_v, cache_b_logf, cache_c_k, cache_c_v, norm_ffn1, ffn1_gate, ffn1_up, ffn1_down, norm_mix, norm_ffn2, ffn2_gate, ffn2_up, ffn2_down, a_w_qkv, a_w_o, a_rel_bias, b_w_qkv, b_w_o, b_w_f, b_b_f, c_w_qkv, c_w_o, norm_final):
    b, s, d = x_prompt.shape
    nb, l, _ = x_sample.shape
    depth = norm_ffn1.shape[0]
    assert d == N_HEADS * HEAD_DIM and s % CAUSAL_TILE == 0 and s >= A_WINDOW and A_WINDOW % BAND_Q_TILE == 0

    xp = x_prompt.reshape(b * s, d)
    xs = x_sample.reshape(nb * l, d)
    heads = (N_HEADS, HEAD_DIM)
    order = ("a_kp", "a_vp", "a_ks", "a_vs", "b_kp", "b_vp", "b_fp", "b_ks", "b_vs", "b_fs",
             "c_kp", "c_vp", "c_ks", "c_vs")
    outs = {name: [] for name in order}
    w1 = (ffn1_gate.astype(BF16), ffn1_up.astype(BF16), ffn1_down.astype(BF16))
    w2 = (ffn2_gate.astype(BF16), ffn2_up.astype(BF16), ffn2_down.astype(BF16))
    w_qkv = (a_w_qkv.astype(BF16), b_w_qkv.astype(BF16), c_w_qkv.astype(BF16))
    w_out = (a_w_o.astype(BF16), b_w_o.astype(BF16), c_w_o.astype(BF16))
    forget_w = (jnp.pad(b_w_f, ((0, 0), (0, 0), (0, LANE - N_HEADS))).astype(BF16),
                jnp.pad(b_b_f, ((0, 0), (0, LANE - N_HEADS))).reshape(-1, 1, LANE))
    assert min(A_WINDOW, s) == ROW_TILE
    caches_t = {name: _cache_t(c) for name, c in (("ak", cache_a_k), ("av", cache_a_v), ("bk", cache_b_k),
                                                  ("bv", cache_b_v), ("ck", cache_c_k), ("cv", cache_c_v))}

    def cache_order(t):
        return jnp.transpose(t.reshape(t.shape[0], N_HEADS, HEAD_DIM, t.shape[2]), (0, 3, 1, 2))

    for i in range(depth):
        kind, slot = i % 3, i // 3
        xp = _ffn(xp, i, norm_ffn1, *w1)
        xs = _ffn(xs, i, norm_ffn1, *w1)
        forget = forget_w if kind == 1 else None
        qp, kpt, vpt, kpb, vptb, *rest_p = _qkv(xp, i, slot, norm_mix, w_qkv[kind], forget, seq=s,
                                                keep_last=kind == 0)
        qs, ks, vs, ksb, vsb, *rest_s = _qkv(xs, i, slot, norm_mix, w_qkv[kind], forget)
        if kind == 0:
            base = _band_base(a_rel_bias[slot], BAND_Q_TILE, BAND_KEYS, BIAS_WIDTH)
            base = base.reshape(N_HEAD_PAIRS, HEADS_PER_STEP, BIAS_WIDTH)
            op = _band_prompt(qp, kpb, vptb, base, b, s)
            n_cache = cache_a_k.shape[2]
            bias_s = _sample_bias(a_rel_bias[slot], l, n_cache).T
            bias_c, bias_n = bias_s[:, :n_cache], bias_s[:, n_cache:]
            os_ = _sample_call(_band_sample_kernel, "band_sample", qs, ksb, vsb, caches_t["ak"], caches_t["av"],
                               slot, extra=(bias_c, bias_n),
                               extra_specs=(_resident(bias_c.shape), _resident(bias_n.shape)))
            outs["a_kp"].append(cache_order(rest_p[0]))
            outs["a_vp"].append(cache_order(rest_p[1]))
            outs["a_ks"].append(ks.reshape(nb, l, *heads))
            outs["a_vs"].append(vs.reshape(nb, l, *heads))
        elif kind == 1:
            fp, fs = rest_p[0], rest_s[0]
            op = _fox_prompt(qp, kpb, vptb, _cumsum_rows(fp, s), b, s)
            past = cache_b_k.shape[2]
            logf_t = jnp.swapaxes(cache_b_logf, 2, 3)
            os_ = _sample_call(_fox_sample_kernel, "fox_sample", qs, ksb, vsb, caches_t["bk"], caches_t["bv"],
                               slot, extra=(logf_t, fs),
                               extra_specs=(pl.BlockSpec((1, 1, N_HEADS, past), lambda bi: (slot, bi, 0, 0)),
                                            pl.BlockSpec((l, LANE), lambda bi: (bi, 0))))
            outs["b_kp"].append(cache_order(kpt))
            outs["b_vp"].append(cache_order(vpt))
            outs["b_fp"].append(fp[:, :N_HEADS].reshape(b, s, N_HEADS))
            outs["b_ks"].append(ks.reshape(nb, l, *heads))
            outs["b_vs"].append(vs.reshape(nb, l, *heads))
            outs["b_fs"].append(fs[:, :N_HEADS].reshape(nb, l, N_HEADS))
        else:
            op = _stick_prompt(qp, kpb, vptb, b, s)
            os_ = _sample_call(_stick_sample_kernel, "stick_sample", qs, ksb, vsb, caches_t["ck"], caches_t["cv"],
                               slot)
            outs["c_kp"].append(cache_order(kpt))
            outs["c_vp"].append(cache_order(vpt))
            outs["c_ks"].append(ks.reshape(nb, l, *heads))
            outs["c_vs"].append(vs.reshape(nb, l, *heads))
        final = norm_final if i == depth - 1 else None
        xp = _ffn(xp, i, norm_ffn2, *w2, proj=(op, w_out[kind], slot), final_g=final)
        xs = _ffn(xs, i, norm_ffn2, *w2, proj=(os_, w_out[kind], slot), final_g=final)

    return (xp.reshape(b, s, d), xs.reshape(nb, l, d)) + tuple(jnp.stack(outs[name]) for name in order)
```

```python
import functools

import jax
import jax.numpy as jnp
from jax import lax
from jax.experimental import pallas as pl
from jax.experimental.pallas import tpu as pltpu

F32 = jnp.float32
BF16 = jnp.bfloat16

N_HEADS = 16
HEAD_DIM = 64
CHUNK = 64
LEFT_CHUNKS = 8
A_WINDOW = LEFT_CHUNKS * CHUNK
REL_MAX = 256
RMS_EPS = 1e-6
ATTN_SCALE = HEAD_DIM ** -0.5
NEG_INF = -1e30

LANE = 128
HEADS_PER_STEP = LANE // HEAD_DIM
N_HEAD_PAIRS = N_HEADS // HEADS_PER_STEP
VMEM_LIMIT = 56 * 1024 * 1024

ROW_TILE = 512
FF_TILE = 256
KV_TILE = 256
CAUSAL_TILE = 512
BAND_Q_TILE = 256
BAND_TILES_PER_STEP = 8
BAND_KEYS = BAND_Q_TILE + A_WINDOW
BIAS_WIDTH = 1024


def _params(*sem):
    return pltpu.CompilerParams(dimension_semantics=sem, vmem_limit_bytes=VMEM_LIMIT)


def _rms(x, g):
    y = x * lax.rsqrt(jnp.mean(x * x, axis=-1, keepdims=True) + RMS_EPS)
    return y * g


def _log_sigmoid(z):
    return jnp.minimum(z, 0.0) - jnp.log(1.0 + jnp.exp(-jnp.abs(z)))


def _dot(a, b):
    return jnp.dot(a, b, preferred_element_type=F32)


def _dot_nt(a, b):
    return lax.dot_general(a, b, (((1,), (1,)), ((), ())), preferred_element_type=F32)


def _split2(x):
    hi = x.astype(BF16)
    lo = (x - hi.astype(F32)).astype(BF16)
    return hi, lo


def _split3(x):
    hi, lo = _split2(x)
    lo2 = (x - hi.astype(F32) - lo.astype(F32)).astype(BF16)
    return hi, lo, lo2


def _resident(shape):
    return pl.BlockSpec(shape, lambda *_: (0,) * len(shape), pipeline_mode=pl.Buffered(1))


def _resident_layer(shape, layer):
    return pl.BlockSpec((None,) + tuple(shape[1:]), lambda *_: (layer,) + (0,) * (len(shape) - 1),
                        pipeline_mode=pl.Buffered(1))


def _ffn_kernel(*refs, has_proj, has_final):
    refs = list(refs)
    x_ref = refs.pop(0)
    if has_proj:
        o_ref, wo_ref = refs.pop(0), refs.pop(0)
    g_ref, wg_ref, wu_ref, wd_ref = refs[:4]
    refs = refs[4:]
    if has_final:
        gf_ref = refs.pop(0)
    out_ref = refs.pop(0)

    x = x_ref[...]
    if has_proj:
        x = x + _dot(o_ref[...], wo_ref[...])
    h = _rms(x, g_ref[...]).astype(BF16)
    d_ff = wg_ref.shape[1]
    acc = jnp.zeros(x.shape, F32)
    for c in range(d_ff // FF_TILE):
        sl = slice(c * FF_TILE, (c + 1) * FF_TILE)
        gate = _dot(h, wg_ref[:, sl])
        up = _dot(h, wu_ref[:, sl])
        act = (gate * jax.nn.sigmoid(gate) * up).astype(BF16)
        acc = acc + _dot(act, wd_ref[sl, :])
    y = x + 0.5 * acc
    if has_final:
        y = _rms(y, gf_ref[...])
    out_ref[...] = y


def _ffn(x, layer, g, wg, wu, wd, proj=None, final_g=None):
    n, d = x.shape
    d_ff = wg.shape[2]
    tm = min(ROW_TILE, n)
    assert n % tm == 0 and d_ff % FF_TILE == 0
    row = pl.BlockSpec((tm, d), lambda i: (i, 0))
    args, specs = [x], [row]
    if proj is not None:
        o, w_o, slot = proj
        args += [o, w_o]
        specs += [row, _resident_layer(w_o.shape, slot)]
    g = g.reshape(g.shape[0], 1, d)
    args += [g, wg, wu, wd]
    specs += [_resident_layer(a.shape, layer) for a in (g, wg, wu, wd)]
    if final_g is not None:
        args.append(final_g.reshape(1, d))
        specs.append(_resident((1, d)))
    return pl.pallas_call(
        functools.partial(_ffn_kernel, has_proj=proj is not None, has_final=final_g is not None),
        grid=(n // tm,),
        in_specs=specs,
        out_specs=row,
        out_shape=jax.ShapeDtypeStruct((n, d), F32),
        compiler_params=_params("parallel"),
        name="ffn",
    )(*args)


def _qkv_kernel(*refs, has_forget, transposed, has_keep):
    x_ref, g_ref, w_ref = refs[:3]
    refs = list(refs[3:])
    if has_forget:
        wf_ref, bf_ref = refs[:2]
        refs = refs[2:]
    q_ref, k_ref, v_ref, kb_ref, vb_ref = refs[:5]
    refs = refs[5:]
    d = x_ref.shape[1]
    h = _rms(x_ref[...], g_ref[...]).astype(BF16)
    q_ref[...] = (_dot(h, w_ref[:, :d]) * ATTN_SCALE).astype(BF16)
    k = _dot(h, w_ref[:, d:2 * d])
    v = _dot(h, w_ref[:, 2 * d:])
    kb_ref[...] = k.astype(BF16)
    if transposed:
        kt, vt = k.T, v.T
        k_ref[0] = kt
        v_ref[0] = vt
        vtb = vt.astype(BF16)
        for t in range(vb_ref.shape[0]):
            vb_ref[t] = vtb[:, t * KV_TILE:(t + 1) * KV_TILE]
    else:
        k_ref[...] = k.reshape(k_ref.shape)
        v_ref[...] = v.reshape(v_ref.shape)
        vb_ref[...] = v.astype(BF16)
    if has_forget:
        refs.pop(0)[...] = _log_sigmoid(_dot(h, wf_ref[...]) + bf_ref[...])
    if has_keep:
        refs[0][...] = kt
        refs[1][...] = vt


def _qkv(x, layer, slot, g, w, forget=None, seq=None, keep_last=False):
    n, d = x.shape
    tm = min(ROW_TILE, n)
    assert n % tm == 0 and tm % KV_TILE == 0
    row = pl.BlockSpec((tm, d), lambda i: (i, 0))
    g = g.reshape(g.shape[0], 1, d)
    args = [x, g, w]
    specs = [row, _resident_layer(g.shape, layer), _resident_layer(w.shape, slot)]
    if seq is not None:
        assert seq % tm == 0 and n % seq == 0
        per = seq // tm
        kv_shape = jax.ShapeDtypeStruct((n // seq, d, seq), F32)
        kv_spec = pl.BlockSpec((1, d, tm), lambda i: (i // per, 0, i % per))
        vb_shape = jax.ShapeDtypeStruct((n // KV_TILE, d, KV_TILE), BF16)
        vb_spec = pl.BlockSpec((tm // KV_TILE, d, KV_TILE), lambda i: (i, 0, 0))
    else:
        kv_shape = jax.ShapeDtypeStruct((n, N_HEADS, HEAD_DIM), F32)
        kv_spec = pl.BlockSpec((tm, N_HEADS, HEAD_DIM), lambda i: (i, 0, 0))
        vb_shape, vb_spec = jax.ShapeDtypeStruct((n, d), BF16), row
    shapes = [jax.ShapeDtypeStruct((n, d), BF16), kv_shape, kv_shape, jax.ShapeDtypeStruct((n, d), BF16), vb_shape]
    out_specs = [row, kv_spec, kv_spec, row, vb_spec]
    if forget is not None:
        args += list(forget)
        specs += [_resident_layer(a.shape, slot) for a in forget]
        shapes.append(jax.ShapeDtypeStruct((n, LANE), F32))
        out_specs.append(pl.BlockSpec((tm, LANE), lambda i: (i, 0)))
    if keep_last:
        keep_spec = pl.BlockSpec((None, d, tm), lambda i: (i // per, 0, 0))
        shapes += [jax.ShapeDtypeStruct((n // seq, d, tm), F32)] * 2
        out_specs += [keep_spec, keep_spec]
    return pl.pallas_call(
        functools.partial(_qkv_kernel, has_forget=forget is not None, transposed=seq is not None,
                          has_keep=keep_last),
        grid=(n // tm,),
        in_specs=specs,
        out_specs=out_specs,
        out_shape=shapes,
        compiler_params=_params("arbitrary" if keep_last else "parallel"),
        name="qkv",
    )(*args)


def _pair_operands(q):
    lane_head = lax.broadcasted_iota(jnp.int32, q.shape, 1) // HEAD_DIM
    return [jnp.where(lane_head == j, q, jnp.zeros_like(q)) for j in range(HEADS_PER_STEP)]


def _vt_tile(vt_ref, t, j, with_ones):
    vt = vt_ref[t]
    if not with_ones:
        return vt
    row_head = lax.broadcasted_iota(jnp.int32, vt.shape, 0) // HEAD_DIM
    return jnp.where(row_head == j, vt, jnp.ones_like(vt))


def _pv_t(vt_ref, first_tile, p, j, with_ones):
    out = None
    for t in range(p.shape[0] // KV_TILE):
        part = _dot(_vt_tile(vt_ref, first_tile + t, j, with_ones), p[t * KV_TILE:(t + 1) * KV_TILE])
        out = part if out is None else out + part
    return out


def _merge_heads_t(per_head, normalise):
    parts = []
    for j, acc in enumerate(per_head):
        own = acc[j * HEAD_DIM:(j + 1) * HEAD_DIM]
        if normalise:
            other = (1 - j) * HEAD_DIM
            own = own / acc[other:other + 1]
        parts.append(own)
    return jnp.concatenate(parts, axis=0).T.astype(BF16)


def _causal_specs(s):
    seq = pl.BlockSpec((s, LANE), lambda b, hp: (b, hp))
    return seq, pl.BlockSpec((s // KV_TILE, LANE, KV_TILE), lambda b, hp: (b, hp, 0))


def _fox_prompt_kernel(q_ref, k_ref, vt_ref, cum_ref, o_ref, cumrep_ref, s_ref, p_ref):
    hp = pl.program_id(1)
    tq = CAUSAL_TILE
    n_kv = tq // KV_TILE
    c = cum_ref[...]
    lane = lax.broadcasted_iota(jnp.int32, c.shape, 1)
    for j in range(HEADS_PER_STEP):
        col = jnp.sum(jnp.where(lane == hp * HEADS_PER_STEP + j, c, 0.0), axis=1, keepdims=True)
        cumrep_ref[j] = jnp.broadcast_to(col, c.shape)
    tri = (lax.broadcasted_iota(jnp.int32, (KV_TILE, KV_TILE), 0)
           <= lax.broadcasted_iota(jnp.int32, (KV_TILE, KV_TILE), 1))
    for qi in range(q_ref.shape[0] // tq):
        slot = qi % 2
        q_rows = slice(qi * tq, (qi + 1) * tq)
        q_pad = _pair_operands(q_ref[q_rows, :])
        blocks = [(kb * tq, tq, 0) for kb in range(qi)]
        blocks += [(qi * tq + t * KV_TILE, KV_TILE, t * KV_TILE) for t in range(n_kv)]
        col_max = []
        for j in range(HEADS_PER_STEP):
            cmax = jnp.full((1, tq), NEG_INF, F32)
            for r0, nr, c0 in blocks:
                rows = slice(r0, r0 + nr)
                s = _dot_nt(k_ref[rows, :], q_pad[j][c0:])
                s = s - jnp.concatenate([cumrep_ref[j, rows, :]] * ((tq - c0) // LANE), axis=1)
                if r0 >= qi * tq:
                    masked = jnp.where(tri, s[:, :KV_TILE], NEG_INF)
                    s = masked if c0 + KV_TILE == tq else jnp.concatenate([masked, s[:, KV_TILE:]], axis=1)
                s_ref[slot, j, rows, c0:] = s
                m = jnp.maximum(cmax[:, c0:], jnp.max(s, axis=0, keepdims=True))
                cmax = m if c0 == 0 else jnp.concatenate([cmax[:, :c0], m], axis=1)
            col_max.append(cmax)
        for j in range(HEADS_PER_STEP):
            for r0, nr, c0 in blocks:
                rows = slice(r0, r0 + nr)
                p_ref[slot, j, rows, c0:] = jnp.exp(s_ref[slot, j, rows, c0:] - col_max[j][:, c0:]).astype(BF16)
        per_head = []
        for j in range(HEADS_PER_STEP):
            acc = None
            for r0, nr, c0 in blocks:
                part = _pv_t(vt_ref, r0 // KV_TILE, p_ref[slot, j, r0:r0 + nr, c0:], j, True)
                if c0:
                    part = jnp.concatenate([jnp.zeros((LANE, c0), F32), part], axis=1)
                acc = part if acc is None else acc + part
            per_head.append(acc)
        o_ref[q_rows, :] = _merge_heads_t(per_head, True)


def _fox_prompt(q, k, vt, cum, b, s):
    tq = CAUSAL_TILE
    seq, vt_spec = _causal_specs(s)
    cum_spec = pl.BlockSpec((s, LANE), lambda bi, hp: (bi, 0))
    return pl.pallas_call(
        _fox_prompt_kernel,
        grid=(b, N_HEAD_PAIRS),
        in_specs=[seq, seq, vt_spec, cum_spec],
        out_specs=seq,
        out_shape=jax.ShapeDtypeStruct(q.shape, BF16),
        scratch_shapes=[pltpu.VMEM((HEADS_PER_STEP, s, LANE), F32),
                        pltpu.VMEM((2, HEADS_PER_STEP, s, tq), F32),
                        pltpu.VMEM((2, HEADS_PER_STEP, s, tq), BF16)],
        compiler_params=_params("parallel", "parallel"),
        name="fox_prompt",
    )(q, k, vt, cum)


def _stick_prompt_kernel(q_ref, k_ref, vt_ref, o_ref, later_ref, acc_ref, z_ref, lb_ref, x_ref, first_ref,
                         tail_ref, w_ref):
    tq = CAUSAL_TILE
    n_kv = tq // KV_TILE
    tri = (lax.broadcasted_iota(jnp.int32, (KV_TILE, KV_TILE), 0)
           < lax.broadcasted_iota(jnp.int32, (KV_TILE, KV_TILE), 1))
    after = jnp.where(lax.broadcasted_iota(jnp.int32, (KV_TILE, KV_TILE), 1)
                      > lax.broadcasted_iota(jnp.int32, (KV_TILE, KV_TILE), 0), 1.0, 0.0).astype(BF16)
    after2 = jnp.concatenate([after, after], axis=1)

    def on_diagonal(x, fill):
        masked = jnp.where(tri, x[:, :KV_TILE], fill)
        return masked if x.shape[1] == KV_TILE else jnp.concatenate([masked, x[:, KV_TILE:]], axis=1)

    pair_no = 0
    for qi in range(q_ref.shape[0] // tq):
        q_rows = slice(qi * tq, (qi + 1) * tq)
        q_pad = _pair_operands(q_ref[q_rows, :])
        later_ref[...] = jnp.zeros(later_ref.shape, F32)
        acc_ref[...] = jnp.zeros(acc_ref.shape, F32)
        for kb in reversed(range(qi + 1)):
            slot = pair_no % 2
            pair_no += 1
            diagonal = kb == qi
            blocks = [(t, slice(t * KV_TILE, (t + 1) * KV_TILE), t * KV_TILE if diagonal else 0)
                      for t in range(n_kv)]
            for j in range(HEADS_PER_STEP):
                for t, rows, c0 in blocks:
                    k = k_ref[kb * tq + t * KV_TILE:kb * tq + (t + 1) * KV_TILE, :]
                    z_ref[slot, j, rows, c0:] = _dot_nt(k, q_pad[j][c0:])
            for j in range(HEADS_PER_STEP):
                for t, rows, c0 in blocks:
                    z = z_ref[slot, j, rows, c0:]
                    log_beta = _log_sigmoid(z)
                    log_keep = log_beta - z
                    if diagonal:
                        log_keep = on_diagonal(log_keep, 0.0)
                    lb_ref[slot, j, rows, c0:] = log_beta
                    hi, lo = _split2(log_keep)
                    x_ref[slot, j, t, :KV_TILE, c0:] = hi
                    x_ref[slot, j, t, KV_TILE:, c0:] = lo
                    first_ref[slot, j, t, :, c0:] = log_keep[:1]
            for j in range(HEADS_PER_STEP):
                later = later_ref[j]
                for t, rows, c0 in reversed(blocks):
                    inner = _dot(after2, x_ref[slot, j, t, :, c0:])
                    tail_ref[slot, j, rows, c0:] = inner + later[:, c0:]
                    grown = later[:, c0:] + (inner[:1] + first_ref[slot, j, t, :, c0:])
                    later = grown if c0 == 0 else jnp.concatenate([later[:, :c0], grown], axis=1)
                later_ref[j] = later
            for j in range(HEADS_PER_STEP):
                for t, rows, c0 in blocks:
                    w = jnp.exp(lb_ref[slot, j, rows, c0:] + tail_ref[slot, j, rows, c0:])
                    if diagonal:
                        w = on_diagonal(w, 0.0)
                    w_ref[slot, j, rows, c0:] = w.astype(BF16)
            for j in range(HEADS_PER_STEP):
                for t, rows, c0 in blocks:
                    part = _dot(_vt_tile(vt_ref, kb * n_kv + t, j, False), w_ref[slot, j, rows, c0:])
                    acc_ref[j, :, c0:] += part
        o_ref[q_rows, :] = _merge_heads_t([acc_ref[j] for j in range(HEADS_PER_STEP)], False)


def _stick_prompt(q, k, vt, b, s):
    tq = CAUSAL_TILE
    seq, vt_spec = _causal_specs(s)
    stage = (2, HEADS_PER_STEP)
    return pl.pallas_call(
        _stick_prompt_kernel,
        grid=(b, N_HEAD_PAIRS),
        in_specs=[seq, seq, vt_spec],
        out_specs=seq,
        out_shape=jax.ShapeDtypeStruct(q.shape, BF16),
        scratch_shapes=[pltpu.VMEM((HEADS_PER_STEP, 1, tq), F32),
                        pltpu.VMEM((HEADS_PER_STEP, LANE, tq), F32),
                        pltpu.VMEM(stage + (tq, tq), F32),
                        pltpu.VMEM(stage + (tq, tq), F32),
                        pltpu.VMEM(stage + (tq // KV_TILE, 2 * KV_TILE, tq), BF16),
                        pltpu.VMEM(stage + (tq // KV_TILE, 1, tq), F32),
                        pltpu.VMEM(stage + (tq, tq), F32),
                        pltpu.VMEM(stage + (tq, tq), BF16)],
        compiler_params=_params("parallel", "parallel"),
        name="stick_prompt",
    )(q, k, vt)


def _band_prompt_kernel(q_ref, k_ref, vt_ref, base_ref, o_ref, bias_ref, s_ref, p_ref):
    b, qg = pl.program_id(1), pl.program_id(2)
    tq = BAND_Q_TILE
    n_sub = q_ref.shape[0] // tq

    @pl.when((b == 0) & (qg == 0))
    def _():
        i = lax.broadcasted_iota(jnp.int32, (tq, BAND_KEYS), 0) // CHUNK
        m = lax.broadcasted_iota(jnp.int32, (tq, BAND_KEYS), 1) // CHUNK
        in_band = (m >= i) & (m <= i + LEFT_CHUNKS)
        for j in range(HEADS_PER_STEP):
            rows = jnp.broadcast_to(base_ref[0, j:j + 1, :], (tq, BIAS_WIDTH))
            rolled = pltpu.roll(rows, 0, 1, stride=1, stride_axis=0)
            bias_ref[j] = jnp.where(in_band, rolled[:, :BAND_KEYS], NEG_INF).T

    def attend(tiles):
        maxima = []
        for sub, (first_tile, n_tiles) in enumerate(tiles):
            q_pad = _pair_operands(q_ref[sub * tq:(sub + 1) * tq, :])
            width = n_tiles * KV_TILE
            start = first_tile * KV_TILE
            if not isinstance(start, int):
                start = pl.multiple_of(start, KV_TILE)
            k = k_ref[pl.ds(start, width), :]
            for j in range(HEADS_PER_STEP):
                s = _dot_nt(k, q_pad[j]) + bias_ref[j, BAND_KEYS - width:, :]
                s_ref[sub * HEADS_PER_STEP + j, :width, :] = s
                maxima.append(jnp.max(s, axis=0, keepdims=True))
        for sub, (first_tile, n_tiles) in enumerate(tiles):
            width = n_tiles * KV_TILE
            for j in range(HEADS_PER_STEP):
                c = sub * HEADS_PER_STEP + j
                p_ref[c, :width, :] = jnp.exp(s_ref[c, :width, :] - maxima[c]).astype(BF16)
        for sub, (first_tile, n_tiles) in enumerate(tiles):
            width = n_tiles * KV_TILE
            per_head = [_pv_t(vt_ref, first_tile, p_ref[sub * HEADS_PER_STEP + j, :width, :], j, True)
                        for j in range(HEADS_PER_STEP)]
            o_ref[sub * tq:(sub + 1) * tq, :] = _merge_heads_t(per_head, True)

    n_lead = A_WINDOW // tq
    band_tiles = BAND_KEYS // KV_TILE
    n_lead_steps = -(-n_lead // n_sub)
    for g in range(n_lead_steps):
        lead = [(0, t + 1) if t < n_lead else (t - n_lead, band_tiles) for t in range(g * n_sub, (g + 1) * n_sub)]
        pl.when(qg == g)(functools.partial(attend, lead))

    @pl.when(qg >= n_lead_steps)
    def _():
        attend([(qg * n_sub + sub - n_lead, band_tiles) for sub in range(n_sub)])


def _band_prompt(q, k, vt, base, b, s):
    tq = BAND_Q_TILE * BAND_TILES_PER_STEP
    nq = s // tq
    q_spec = pl.BlockSpec((tq, LANE), lambda hp, bi, qi: (bi * nq + qi, hp))
    k_spec = pl.BlockSpec((s, LANE), lambda hp, bi, qi: (bi, hp))
    vt_spec = pl.BlockSpec((s // KV_TILE, LANE, KV_TILE), lambda hp, bi, qi: (bi, hp, 0))
    base_spec = pl.BlockSpec((1, HEADS_PER_STEP, BIAS_WIDTH), lambda hp, bi, qi: (hp, 0, 0))
    return pl.pallas_call(
        _band_prompt_kernel,
        grid=(N_HEAD_PAIRS, b, s // tq),
        in_specs=[q_spec, k_spec, vt_spec, base_spec],
        out_specs=q_spec,
        out_shape=jax.ShapeDtypeStruct(q.shape, BF16),
        scratch_shapes=[pltpu.VMEM((HEADS_PER_STEP, BAND_KEYS, BAND_Q_TILE), F32),
                        pltpu.VMEM((BAND_TILES_PER_STEP * HEADS_PER_STEP, BAND_KEYS, BAND_Q_TILE), F32),
                        pltpu.VMEM((BAND_TILES_PER_STEP * HEADS_PER_STEP, BAND_KEYS, BAND_Q_TILE), BF16)],
        compiler_params=_params("arbitrary", "arbitrary", "arbitrary"),
        name="band_prompt",
    )(q, k, vt, base)


def _rel_rows_desc(rel_table, hi, lo):
    top = jnp.broadcast_to(rel_table[-1], (max(0, hi - max(lo, REL_MAX + 1) + 1), rel_table.shape[1]))
    mid = rel_table[max(lo, -REL_MAX) + REL_MAX:min(hi, REL_MAX) + REL_MAX + 1][::-1]
    bot = jnp.broadcast_to(rel_table[0], (max(0, min(hi, -REL_MAX - 1) - lo + 1), rel_table.shape[1]))
    return jnp.concatenate([top, mid, bot], axis=0)


def _band_base(rel_table, q_rows, n_keys, width):
    offset = n_keys - q_rows
    desc = _rel_rows_desc(rel_table, offset + q_rows - 1, offset - n_keys + 1)
    pad = jnp.zeros((width - desc.shape[0], rel_table.shape[1]), rel_table.dtype)
    return jnp.concatenate([desc[q_rows - 1:], pad, desc[:q_rows - 1]], axis=0).T


def _sample_bias(rel_table, l, n_cache):
    n = n_cache + l
    desc = _rel_rows_desc(rel_table, n_cache + l - 1, -(l - 1))
    cols = jnp.stack([desc[l - 1 - i:l - 1 - i + n] for i in range(l)], axis=1)
    return jnp.swapaxes(cols, 1, 2).reshape(n, rel_table.shape[1] * l)


def _block_diag_q(q, l):
    d = q.shape[1]
    tiled = jnp.concatenate([q] * N_HEADS, axis=0)
    r = lax.broadcasted_iota(jnp.int32, (N_HEADS * l, d), 0) // l
    c = lax.broadcasted_iota(jnp.int32, (N_HEADS * l, d), 1) // HEAD_DIM
    return jnp.where(r == c, tiled, jnp.zeros_like(tiled))


def _gather_heads(r, l):
    d = r.shape[1]
    rr = lax.broadcasted_iota(jnp.int32, (N_HEADS * l, d), 0) // l
    c = lax.broadcasted_iota(jnp.int32, (N_HEADS * l, d), 1) // HEAD_DIM
    r = jnp.where(rr == c, r, 0.0)
    out = r[:l]
    for h in range(1, N_HEADS):
        out = out + r[h * l:(h + 1) * l]
    return out


def _new_key_masks(l, strict):
    i = lax.broadcasted_iota(jnp.int32, (N_HEADS * l, l), 0) % l
    j = lax.broadcasted_iota(jnp.int32, (N_HEADS * l, l), 1)
    return (j < i) if strict else (j <= i)


def _sample_scores(q_ref, kn_ref, kt_ref):
    l = q_ref.shape[0]
    qb = _block_diag_q(q_ref[...], l)
    return _dot(qb, kt_ref[0, 0].astype(BF16)), _dot_nt(qb, kn_ref[...]), l


def _sample_output(w_cache, w_new, vt_ref, vn_ref, l):
    r = _dot_nt(w_cache.astype(BF16), vt_ref[0, 0].astype(BF16)) + _dot(w_new.astype(BF16), vn_ref[...])
    return _gather_heads(r, l).astype(BF16)


def _softmax_two(s_cache, s_new):
    m = jnp.maximum(jnp.max(s_cache, axis=1, keepdims=True), jnp.max(s_new, axis=1, keepdims=True))
    p_cache, p_new = jnp.exp(s_cache - m), jnp.exp(s_new - m)
    inv = 1.0 / (jnp.sum(p_cache, axis=1, keepdims=True) + jnp.sum(p_new, axis=1, keepdims=True))
    return p_cache * inv, p_new * inv


def _col_blocks(x):
    n = x.shape[1]
    return [x[:, c:min(c + KV_TILE, n)] for c in range(0, n, KV_TILE)]


def _prefix_sums_cols(blocks):
    upto = jnp.where(lax.broadcasted_iota(jnp.int32, (KV_TILE, KV_TILE), 0)
                     <= lax.broadcasted_iota(jnp.int32, (KV_TILE, KV_TILE), 1), 1.0, 0.0).astype(BF16)
    carry = jnp.zeros((blocks[0].shape[0], 1), F32)
    out = []
    for x in blocks:
        m = x.shape[1]
        c = sum(_dot(part, upto[:m, :m]) for part in _split3(x)) + carry
        out.append(c)
        carry = c[:, m - 1:]
    return out


def _later_sums_cols(blocks):
    after = jnp.where(lax.broadcasted_iota(jnp.int32, (KV_TILE, KV_TILE), 0)
                      > lax.broadcasted_iota(jnp.int32, (KV_TILE, KV_TILE), 1), 1.0, 0.0).astype(BF16)
    carry = jnp.zeros((blocks[0].shape[0], 1), F32)
    out = []
    for x in reversed(blocks):
        m = x.shape[1]
        inner = sum(_dot(part, after[:m, :m]) for part in _split2(x))
        out.append(inner + carry)
        carry = carry + (inner[:, :1] + x[:, :1])
    return out[::-1]


def _band_sample_kernel(q_ref, kn_ref, vn_ref, kt_ref, vt_ref, bias_c_ref, bias_n_ref, o_ref):
    s_cache, s_new, l = _sample_scores(q_ref, kn_ref, kt_ref)
    p_cache, p_new = _softmax_two(s_cache + bias_c_ref[...], s_new + bias_n_ref[...])
    o_ref[...] = _sample_output(p_cache, p_new, vt_ref, vn_ref, l)


def _fox_sample_kernel(q_ref, kn_ref, vn_ref, kt_ref, vt_ref, fc_ref, fn_ref, o_ref):
    s_cache, s_new, l = _sample_scores(q_ref, kn_ref, kt_ref)
    eh = lax.broadcasted_iota(jnp.int32, (N_HEADS * l, N_HEADS), 0) // l
    ec = lax.broadcasted_iota(jnp.int32, (N_HEADS * l, N_HEADS), 1)
    expand = jnp.where(eh == ec, 1.0, 0.0).astype(BF16)
    spread_c = sum(_dot(expand, part) for part in _split3(fc_ref[0, 0]))
    spread_n = sum(_dot_nt(expand, part) for part in _split3(fn_ref[:, :N_HEADS]))
    cums = _prefix_sums_cols(_col_blocks(spread_c) + [spread_n])
    s_cache = s_cache - jnp.concatenate(cums[:-1], axis=1)
    s_new = jnp.where(_new_key_masks(l, strict=False), s_new - cums[-1], NEG_INF)
    p_cache, p_new = _softmax_two(s_cache, s_new)
    o_ref[...] = _sample_output(p_cache, p_new, vt_ref, vn_ref, l)


def _stick_sample_kernel(q_ref, kn_ref, vn_ref, kt_ref, vt_ref, o_ref):
    z_cache, z_new, l = _sample_scores(q_ref, kn_ref, kt_ref)
    earlier = _new_key_masks(l, strict=True)
    lb_cache, lb_new = _log_sigmoid(z_cache), _log_sigmoid(z_new)
    keep_new = jnp.where(earlier, lb_new - z_new, 0.0)
    tails = _later_sums_cols(_col_blocks(lb_cache - z_cache) + [keep_new])
    w_cache = jnp.exp(lb_cache + jnp.concatenate(tails[:-1], axis=1))
    w_new = jnp.where(earlier, jnp.exp(lb_new + tails[-1]), 0.0)
    o_ref[...] = _sample_output(w_cache, w_new, vt_ref, vn_ref, l)


def _cache_t(cache):
    n, nb, past, nh, hd = cache.shape
    return jnp.transpose(cache, (0, 1, 3, 4, 2)).reshape(n, nb, nh * hd, past)


def _sample_call(kernel, name, q, k_new, v_new, cache_kt, cache_vt, slot, extra=(), extra_specs=()):
    _, nb, d, past = cache_kt.shape
    l = q.shape[0] // nb
    new = pl.BlockSpec((l, d), lambda b: (b, 0))
    cache = pl.BlockSpec((1, 1, d, past), lambda b: (slot, b, 0, 0))
    return pl.pallas_call(
        kernel,
        grid=(nb,),
        in_specs=[new, new, new, cache, cache, *extra_specs],
        out_specs=new,
        out_shape=jax.ShapeDtypeStruct(q.shape, BF16),
        compiler_params=_params("parallel"),
        name=name,
    )(q, k_new, v_new, cache_kt, cache_vt, *extra)


def _cumsum_kernel(f_ref, o_ref):
    upto = jnp.where(lax.broadcasted_iota(jnp.int32, (KV_TILE, KV_TILE), 1)
                     <= lax.broadcasted_iota(jnp.int32, (KV_TILE, KV_TILE), 0), 1.0, 0.0).astype(BF16)
    carry = jnp.zeros((1, f_ref.shape[1]), F32)
    for t in range(f_ref.shape[0] // KV_TILE):
        rows = slice(t * KV_TILE, (t + 1) * KV_TILE)
        c = sum(_dot(upto, part) for part in _split3(f_ref[rows, :])) + carry
        o_ref[rows, :] = c
        carry = c[KV_TILE - 1:, :]


def _cumsum_rows(log_f, s):
    spec = pl.BlockSpec((s, log_f.shape[1]), lambda i: (i, 0))
    return pl.pallas_call(
        _cumsum_kernel, grid=(log_f.shape[0] // s,), in_specs=[spec], out_specs=spec,
        out_shape=jax.ShapeDtypeStruct(log_f.shape, F32),
        compiler_params=_params("parallel"), name="forget_cumsum",
    )(log_f)


def kernel(x_prompt, x_sample, cache_a_k, cache_a_v, cache_b_k, cache_b_v, cache_b_logf, cache_c_k, cache_c_v, norm_ffn1, ffn1_gate, ffn1_up, ffn1_down, norm_mix, norm_ffn2, ffn2_gate, ffn2_up, ffn2_down, a_w_qkv, a_w_o, a_rel_bias, b_w_qkv, b_w_o, b_w_f, b_b_f, c_w_qkv, c_w_o, norm_final):
    b, s, d = x_prompt.shape
    nb, l, _ = x_sample.shape
    depth = norm_ffn1.shape[0]
    assert d == N_HEADS * HEAD_DIM and s % CAUSAL_TILE == 0 and s >= A_WINDOW and A_WINDOW % BAND_Q_TILE == 0

    xp = x_prompt.reshape(b * s, d)
    xs = x_sample.reshape(nb * l, d)
    heads = (N_HEADS, HEAD_DIM)
    order = ("a_kp", "a_vp", "a_ks", "a_vs", "b_kp", "b_vp", "b_fp", "b_ks", "b_vs", "b_fs",
             "c_kp", "c_vp", "c_ks", "c_vs")
    outs = {name: [] for name in order}
    w1 = (ffn1_gate.astype(BF16), ffn1_up.astype(BF16), ffn1_down.astype(BF16))
    w2 = (ffn2_gate.astype(BF16), ffn2_up.astype(BF16), ffn2_down.astype(BF16))
    w_qkv = (a_w_qkv.astype(BF16), b_w_qkv.astype(BF16), c_w_qkv.astype(BF16))
    w_out = (a_w_o.astype(BF16), b_w_o.astype(BF16), c_w_o.astype(BF16))
    forget_w = (jnp.pad(b_w_f, ((0, 0), (0, 0), (0, LANE - N_HEADS))).astype(BF16),
                jnp.pad(b_b_f, ((0, 0), (0, LANE - N_HEADS))).reshape(-1, 1, LANE))
    assert min(A_WINDOW, s) == ROW_TILE
    caches_t = {name: _cache_t(c) for name, c in (("ak", cache_a_k), ("av", cache_a_v), ("bk", cache_b_k),
                                                  ("bv", cache_b_v), ("ck", cache_c_k), ("cv", cache_c_v))}

    def cache_order(t):
        return jnp.transpose(t.reshape(t.shape[0], N_HEADS, HEAD_DIM, t.shape[2]), (0, 3, 1, 2))

    for i in range(depth):
        kind, slot = i % 3, i // 3
        xp = _ffn(xp, i, norm_ffn1, *w1)
        xs = _ffn(xs, i, norm_ffn1, *w1)
        forget = forget_w if kind == 1 else None
        qp, kpt, vpt, kpb, vptb, *rest_p = _qkv(xp, i, slot, norm_mix, w_qkv[kind], forget, seq=s,
                                                keep_last=kind == 0)
        qs, ks, vs, ksb, vsb, *rest_s = _qkv(xs, i, slot, norm_mix, w_qkv[kind], forget)
        if kind == 0:
            base = _band_base(a_rel_bias[slot], BAND_Q_TILE, BAND_KEYS, BIAS_WIDTH)
            base = base.reshape(N_HEAD_PAIRS, HEADS_PER_STEP, BIAS_WIDTH)
            op = _band_prompt(qp, kpb, vptb, base, b, s)
            n_cache = cache_a_k.shape[2]
            bias_s = _sample_bias(a_rel_bias[slot], l, n_cache).T
            bias_c, bias_n = bias_s[:, :n_cache], bias_s[:, n_cache:]
            os_ = _sample_call(_band_sample_kernel, "band_sample", qs, ksb, vsb, caches_t["ak"], caches_t["av"],
                               slot, extra=(bias_c, bias_n),
                               extra_specs=(_resident(bias_c.shape), _resident(bias_n.shape)))
            outs["a_kp"].append(cache_order(rest_p[0]))
            outs["a_vp"].append(cache_order(rest_p[1]))
            outs["a_ks"].append(ks.reshape(nb, l, *heads))
            outs["a_vs"].append(vs.reshape(nb, l, *heads))
        elif kind == 1:
            fp, fs = rest_p[0], rest_s[0]
            op = _fox_prompt(qp, kpb, vptb, _cumsum_rows(fp, s), b, s)
            past = cache_b_k.shape[2]
            logf_t = jnp.swapaxes(cache_b_logf, 2, 3)
            os_ = _sample_call(_fox_sample_kernel, "fox_sample", qs, ksb, vsb, caches_t["bk"], caches_t["bv"],
                               slot, extra=(logf_t, fs),
                               extra_specs=(pl.BlockSpec((1, 1, N_HEADS, past), lambda bi: (slot, bi, 0, 0)),
                                            pl.BlockSpec((l, LANE), lambda bi: (bi, 0))))
            outs["b_kp"].append(cache_order(kpt))
            outs["b_vp"].append(cache_order(vpt))
            outs["b_fp"].append(fp[:, :N_HEADS].reshape(b, s, N_HEADS))
            outs["b_ks"].append(ks.reshape(nb, l, *heads))
            outs["b_vs"].append(vs.reshape(nb, l, *heads))
            outs["b_fs"].append(fs[:, :N_HEADS].reshape(nb, l, N_HEADS))
        else:
            op = _stick_prompt(qp, kpb, vptb, b, s)
            os_ = _sample_call(_stick_sample_kernel, "stick_sample", qs, ksb, vsb, caches_t["ck"], caches_t["cv"],
                               slot)
            outs["c_kp"].append(cache_order(kpt))
            outs["c_vp"].append(cache_order(vpt))
            outs["c_ks"].append(ks.reshape(nb, l, *heads))
            outs["c_vs"].append(vs.reshape(nb, l, *heads))
        final = norm_final if i == depth - 1 else None
        xp = _ffn(xp, i, norm_ffn2, *w2, proj=(op, w_out[kind], slot), final_g=final)
        xs = _ffn(xs, i, norm_ffn2, *w2, proj=(os_, w_out[kind], slot), final_g=final)

    return (xp.reshape(b, s, d), xs.reshape(nb, l, d)) + tuple(jnp.stack(outs[name]) for name in order)
```

```python
import functools

import jax
import jax.numpy as jnp
from jax import lax
from jax.experimental import pallas as pl
from jax.experimental.pallas import tpu as pltpu

F32 = jnp.float32
BF16 = jnp.bfloat16

N_HEADS = 16
HEAD_DIM = 64
CHUNK = 64
LEFT_CHUNKS = 8
A_WINDOW = LEFT_CHUNKS * CHUNK
REL_MAX = 256
RMS_EPS = 1e-6
ATTN_SCALE = HEAD_DIM ** -0.5
NEG_INF = -1e30

LANE = 128
HEADS_PER_STEP = LANE // HEAD_DIM
N_HEAD_PAIRS = N_HEADS // HEADS_PER_STEP
VMEM_LIMIT = 56 * 1024 * 1024

ROW_TILE = 512
FF_TILE = 256
KV_TILE = 256
CAUSAL_TILE = 512
BAND_Q_TILE = 256
BAND_TILES_PER_STEP = 8
BAND_KEYS = BAND_Q_TILE + A_WINDOW
BIAS_WIDTH = 1024


def _params(*sem):
    return pltpu.CompilerParams(dimension_semantics=sem, vmem_limit_bytes=VMEM_LIMIT)


def _rms(x, g):
    y = x * lax.rsqrt(jnp.mean(x * x, axis=-1, keepdims=True) + RMS_EPS)
    return y * g


def _log_sigmoid(z):
    return jnp.minimum(z, 0.0) - jnp.log(1.0 + jnp.exp(-jnp.abs(z)))


def _dot(a, b):
    return jnp.dot(a, b, preferred_element_type=F32)


def _dot_nt(a, b):
    return lax.dot_general(a, b, (((1,), (1,)), ((), ())), preferred_element_type=F32)


def _split2(x):
    hi = x.astype(BF16)
    lo = (x - hi.astype(F32)).astype(BF16)
    return hi, lo


def _split3(x):
    hi, lo = _split2(x)
    lo2 = (x - hi.astype(F32) - lo.astype(F32)).astype(BF16)
    return hi, lo, lo2


def _resident(shape):
    return pl.BlockSpec(shape, lambda *_: (0,) * len(shape), pipeline_mode=pl.Buffered(1))


def _resident_layer(shape, layer):
    return pl.BlockSpec((None,) + tuple(shape[1:]), lambda *_: (layer,) + (0,) * (len(shape) - 1),
                        pipeline_mode=pl.Buffered(1))


def _ffn_kernel(*refs, has_proj, has_final):
    refs = list(refs)
    x_ref = refs.pop(0)
    if has_proj:
        o_ref, wo_ref = refs.pop(0), refs.pop(0)
    g_ref, wg_ref, wu_ref, wd_ref = refs[:4]
    refs = refs[4:]
    if has_final:
        gf_ref = refs.pop(0)
    out_ref = refs.pop(0)

    x = x_ref[...]
    if has_proj:
        x = x + _dot(o_ref[...], wo_ref[...])
    h = _rms(x, g_ref[...]).astype(BF16)
    d_ff = wg_ref.shape[1]
    acc = jnp.zeros(x.shape, F32)
    for c in range(d_ff // FF_TILE):
        sl = slice(c * FF_TILE, (c + 1) * FF_TILE)
        gate = _dot(h, wg_ref[:, sl])
        up = _dot(h, wu_ref[:, sl])
        act = (gate * jax.nn.sigmoid(gate) * up).astype(BF16)
        acc = acc + _dot(act, wd_ref[sl, :])
    y = x + 0.5 * acc
    if has_final:
        y = _rms(y, gf_ref[...])
    out_ref[...] = y


def _ffn(x, layer, g, wg, wu, wd, proj=None, final_g=None):
    n, d = x.shape
    d_ff = wg.shape[2]
    tm = min(ROW_TILE, n)
    assert n % tm == 0 and d_ff % FF_TILE == 0
    row = pl.BlockSpec((tm, d), lambda i: (i, 0))
    args, specs = [x], [row]
    if proj is not None:
        o, w_o, slot = proj
        args += [o, w_o]
        specs += [row, _resident_layer(w_o.shape, slot)]
    g = g.reshape(g.shape[0], 1, d)
    args += [g, wg, wu, wd]
    specs += [_resident_layer(a.shape, layer) for a in (g, wg, wu, wd)]
    if final_g is not None:
        args.append(final_g.reshape(1, d))
        specs.append(_resident((1, d)))
    return pl.pallas_call(
        functools.partial(_ffn_kernel, has_proj=proj is not None, has_final=final_g is not None),
        grid=(n // tm,),
        in_specs=specs,
        out_specs=row,
        out_shape=jax.ShapeDtypeStruct((n, d), F32),
        compiler_params=_params("parallel"),
        name="ffn",
    )(*args)


def _qkv_kernel(*refs, has_forget, transposed, has_keep):
    x_ref, g_ref, w_ref = refs[:3]
    refs = list(refs[3:])
    if has_forget:
        wf_ref, bf_ref = refs[:2]
        refs = refs[2:]
    q_ref, k_ref, v_ref, kb_ref, vb_ref = refs[:5]
    refs = refs[5:]
    d = x_ref.shape[1]
    h = _rms(x_ref[...], g_ref[...]).astype(BF16)
    q_ref[...] = (_dot(h, w_ref[:, :d]) * ATTN_SCALE).astype(BF16)
    k = _dot(h, w_ref[:, d:2 * d])
    v = _dot(h, w_ref[:, 2 * d:])
    kb_ref[...] = k.astype(BF16)
    if transposed:
        kt, vt = k.T, v.T
        k_ref[0] = kt
        v_ref[0] = vt
        vtb = vt.astype(BF16)
        for t in range(vb_ref.shape[0]):
            vb_ref[t] = vtb[:, t * KV_TILE:(t + 1) * KV_TILE]
    else:
        k_ref[...] = k.reshape(k_ref.shape)
        v_ref[...] = v.reshape(v_ref.shape)
        vb_ref[...] = v.astype(BF16)
    if has_forget:
        refs.pop(0)[...] = _log_sigmoid(_dot(h, wf_ref[...]) + bf_ref[...])
    if has_keep:
        refs[0][...] = kt
        refs[1][...] = vt


def _qkv(x, layer, slot, g, w, forget=None, seq=None, keep_last=False):
    n, d = x.shape
    tm = min(ROW_TILE, n)
    assert n % tm == 0 and tm % KV_TILE == 0
    row = pl.BlockSpec((tm, d), lambda i: (i, 0))
    g = g.reshape(g.shape[0], 1, d)
    args = [x, g, w]
    specs = [row, _resident_layer(g.shape, layer), _resident_layer(w.shape, slot)]
    if seq is not None:
        assert seq % tm == 0 and n % seq == 0
        per = seq // tm
        kv_shape = jax.ShapeDtypeStruct((n // seq, d, seq), F32)
        kv_spec = pl.BlockSpec((1, d, tm), lambda i: (i // per, 0, i % per))
        vb_shape = jax.ShapeDtypeStruct((n // KV_TILE, d, KV_TILE), BF16)
        vb_spec = pl.BlockSpec((tm // KV_TILE, d, KV_TILE), lambda i: (i, 0, 0))
    else:
        kv_shape = jax.ShapeDtypeStruct((n, N_HEADS, HEAD_DIM), F32)
        kv_spec = pl.BlockSpec((tm, N_HEADS, HEAD_DIM), lambda i: (i, 0, 0))
        vb_shape, vb_spec = jax.ShapeDtypeStruct((n, d), BF16), row
    shapes = [jax.ShapeDtypeStruct((n, d), BF16), kv_shape, kv_shape, jax.ShapeDtypeStruct((n, d), BF16), vb_shape]
    out_specs = [row, kv_spec, kv_spec, row, vb_spec]
    if forget is not None:
        args += list(forget)
        specs += [_resident_layer(a.shape, slot) for a in forget]
        shapes.append(jax.ShapeDtypeStruct((n, LANE), F32))
        out_specs.append(pl.BlockSpec((tm, LANE), lambda i: (i, 0)))
    if keep_last:
        keep_spec = pl.BlockSpec((None, d, tm), lambda i: (i // per, 0, 0))
        shapes += [jax.ShapeDtypeStruct((n // seq, d, tm), F32)] * 2
        out_specs += [keep_spec, keep_spec]
    return pl.pallas_call(
        functools.partial(_qkv_kernel, has_forget=forget is not None, transposed=seq is not None,
                          has_keep=keep_last),
        grid=(n // tm,),
        in_specs=specs,
        out_specs=out_specs,
        out_shape=shapes,
        compiler_params=_params("arbitrary" if keep_last else "parallel"),
        name="qkv",
    )(*args)


def _pair_operands(q):
    lane_head = lax.broadcasted_iota(jnp.int32, q.shape, 1) // HEAD_DIM
    return [jnp.where(lane_head == j, q, jnp.zeros_like(q)) for j in range(HEADS_PER_STEP)]


def _vt_tile(vt_ref, t, j, with_ones):
    vt = vt_ref[t]
    if not with_ones:
        return vt
    row_head = lax.broadcasted_iota(jnp.int32, vt.shape, 0) // HEAD_DIM
    return jnp.where(row_head == j, vt, jnp.ones_like(vt))


def _pv_t(vt_ref, first_tile, p, j, with_ones):
    out = None
    for t in range(p.shape[0] // KV_TILE):
        part = _dot(_vt_tile(vt_ref, first_tile + t, j, with_ones), p[t * KV_TILE:(t + 1) * KV_TILE])
        out = part if out is None else out + part
    return out


def _merge_heads_t(per_head, normalise):
    parts = []
    for j, acc in enumerate(per_head):
        own = acc[j * HEAD_DIM:(j + 1) * HEAD_DIM]
        if normalise:
            other = (1 - j) * HEAD_DIM
            own = own / acc[other:other + 1]
        parts.append(own)
    return jnp.concatenate(parts, axis=0).T.astype(BF16)


def _causal_specs(s):
    seq = pl.BlockSpec((s, LANE), lambda b, hp: (b, hp))
    return seq, pl.BlockSpec((s // KV_TILE, LANE, KV_TILE), lambda b, hp: (b, hp, 0))


def _fox_prompt_kernel(q_ref, k_ref, vt_ref, cum_ref, o_ref, cumrep_ref, s_ref, p_ref):
    hp = pl.program_id(1)
    tq = CAUSAL_TILE
    n_kv = tq // KV_TILE
    c = cum_ref[...]
    lane = lax.broadcasted_iota(jnp.int32, c.shape, 1)
    for j in range(HEADS_PER_STEP):
        col = jnp.sum(jnp.where(lane == hp * HEADS_PER_STEP + j, c, 0.0), axis=1, keepdims=True)
        cumrep_ref[j] = jnp.broadcast_to(col, c.shape)
    tri = (lax.broadcasted_iota(jnp.int32, (KV_TILE, KV_TILE), 0)
           <= lax.broadcasted_iota(jnp.int32, (KV_TILE, KV_TILE), 1))
    for qi in range(q_ref.shape[0] // tq):
        slot = qi % 2
        q_rows = slice(qi * tq, (qi + 1) * tq)
        q_pad = _pair_operands(q_ref[q_rows, :])
        blocks = [(kb * tq, tq, 0) for kb in range(qi)]
        blocks += [(qi * tq + t * KV_TILE, KV_TILE, t * KV_TILE) for t in range(n_kv)]
        col_max = []
        for j in range(HEADS_PER_STEP):
            cmax = jnp.full((1, tq), NEG_INF, F32)
            for r0, nr, c0 in blocks:
                rows = slice(r0, r0 + nr)
                s = _dot_nt(k_ref[rows, :], q_pad[j][c0:])
                s = s - jnp.concatenate([cumrep_ref[j, rows, :]] * ((tq - c0) // LANE), axis=1)
                if r0 >= qi * tq:
                    masked = jnp.where(tri, s[:, :KV_TILE], NEG_INF)
                    s = masked if c0 + KV_TILE == tq else jnp.concatenate([masked, s[:, KV_TILE:]], axis=1)
                s_ref[slot, j, rows, c0:] = s
                m = jnp.maximum(cmax[:, c0:], jnp.max(s, axis=0, keepdims=True))
                cmax = m if c0 == 0 else jnp.concatenate([cmax[:, :c0], m], axis=1)
            col_max.append(cmax)
        for j in range(HEADS_PER_STEP):
            for r0, nr, c0 in blocks:
                rows = slice(r0, r0 + nr)
                p_ref[slot, j, rows, c0:] = jnp.exp(s_ref[slot, j, rows, c0:] - col_max[j][:, c0:]).astype(BF16)
        per_head = []
        for j in range(HEADS_PER_STEP):
            acc = None
            for r0, nr, c0 in blocks:
                part = _pv_t(vt_ref, r0 // KV_TILE, p_ref[slot, j, r0:r0 + nr, c0:], j, True)
                if c0:
                    part = jnp.concatenate([jnp.zeros((LANE, c0), F32), part], axis=1)
                acc = part if acc is None else acc + part
            per_head.append(acc)
        o_ref[q_rows, :] = _merge_heads_t(per_head, True)


def _fox_prompt(q, k, vt, cum, b, s):
    tq = CAUSAL_TILE
    seq, vt_spec = _causal_specs(s)
    cum_spec = pl.BlockSpec((s, LANE), lambda bi, hp: (bi, 0))
    return pl.pallas_call(
        _fox_prompt_kernel,
        grid=(b, N_HEAD_PAIRS),
        in_specs=[seq, seq, vt_spec, cum_spec],
        out_specs=seq,
        out_shape=jax.ShapeDtypeStruct(q.shape, BF16),
        scratch_shapes=[pltpu.VMEM((HEADS_PER_STEP, s, LANE), F32),
                        pltpu.VMEM((2, HEADS_PER_STEP, s, tq), F32),
                        pltpu.VMEM((2, HEADS_PER_STEP, s, tq), BF16)],
        compiler_params=_params("parallel", "parallel"),
        name="fox_prompt",
    )(q, k, vt, cum)


def _stick_prompt_kernel(q_ref, k_ref, vt_ref, o_ref, later_ref, acc_ref, z_ref, lb_ref, x_ref, first_ref,
                         tail_ref, w_ref):
    tq = CAUSAL_TILE
    n_kv = tq // KV_TILE
    tri = (lax.broadcasted_iota(jnp.int32, (KV_TILE, KV_TILE), 0)
           < lax.broadcasted_iota(jnp.int32, (KV_TILE, KV_TILE), 1))
    after = jnp.where(lax.broadcasted_iota(jnp.int32, (KV_TILE, KV_TILE), 1)
                      > lax.broadcasted_iota(jnp.int32, (KV_TILE, KV_TILE), 0), 1.0, 0.0).astype(BF16)
    after2 = jnp.concatenate([after, after], axis=1)

    def on_diagonal(x, fill):
        masked = jnp.where(tri, x[:, :KV_TILE], fill)
        return masked if x.shape[1] == KV_TILE else jnp.concatenate([masked, x[:, KV_TILE:]], axis=1)

    pair_no = 0
    for qi in range(q_ref.shape[0] // tq):
        q_rows = slice(qi * tq, (qi + 1) * tq)
        q_pad = _pair_operands(q_ref[q_rows, :])
        later_ref[...] = jnp.zeros(later_ref.shape, F32)
        acc_ref[...] = jnp.zeros(acc_ref.shape, F32)
        for kb in reversed(range(qi + 1)):
            slot = pair_no % 2
            pair_no += 1
            diagonal = kb == qi
            blocks = [(t, slice(t * KV_TILE, (t + 1) * KV_TILE), t * KV_TILE if diagonal else 0)
                      for t in range(n_kv)]
            for j in range(HEADS_PER_STEP):
                for t, rows, c0 in blocks:
                    k = k_ref[kb * tq + t * KV_TILE:kb * tq + (t + 1) * KV_TILE, :]
                    z_ref[slot, j, rows, c0:] = _dot_nt(k, q_pad[j][c0:])
            for j in range(HEADS_PER_STEP):
                for t, rows, c0 in blocks:
                    z = z_ref[slot, j, rows, c0:]
                    log_beta = _log_sigmoid(z)
                    log_keep = log_beta - z
                    if diagonal:
                        log_keep = on_diagonal(log_keep, 0.0)
                    lb_ref[slot, j, rows, c0:] = log_beta
                    hi, lo = _split2(log_keep)
                    x_ref[slot, j, t, :KV_TILE, c0:] = hi
                    x_ref[slot, j, t, KV_TILE:, c0:] = lo
                    first_ref[slot, j, t, :, c0:] = log_keep[:1]
            for j in range(HEADS_PER_STEP):
                later = later_ref[j]
                for t, rows, c0 in reversed(blocks):
                    inner = _dot(after2, x_ref[slot, j, t, :, c0:])
                    tail_ref[slot, j, rows, c0:] = inner + later[:, c0:]
                    grown = later[:, c0:] + (inner[:1] + first_ref[slot, j, t, :, c0:])
                    later = grown if c0 == 0 else jnp.concatenate([later[:, :c0], grown], axis=1)
                later_ref[j] = later
            for j in range(HEADS_PER_STEP):
                for t, rows, c0 in blocks:
                    w = jnp.exp(lb_ref[slot, j, rows, c0:] + tail_ref[slot, j, rows, c0:])
                    if diagonal:
                        w = on_diagonal(w, 0.0)
                    w_ref[slot, j, rows, c0:] = w.astype(BF16)
            for j in range(HEADS_PER_STEP):
                for t, rows, c0 in blocks:
                    part = _dot(_vt_tile(vt_ref, kb * n_kv + t, j, False), w_ref[slot, j, rows, c0:])
                    acc_ref[j, :, c0:] += part
        o_ref[q_rows, :] = _merge_heads_t([acc_ref[j] for j in range(HEADS_PER_STEP)], False)


def _stick_prompt(q, k, vt, b, s):
    tq = CAUSAL_TILE
    seq, vt_spec = _causal_specs(s)
    stage = (2, HEADS_PER_STEP)
    return pl.pallas_call(
        _stick_prompt_kernel,
        grid=(b, N_HEAD_PAIRS),
        in_specs=[seq, seq, vt_spec],
        out_specs=seq,
        out_shape=jax.ShapeDtypeStruct(q.shape, BF16),
        scratch_shapes=[pltpu.VMEM((HEADS_PER_STEP, 1, tq), F32),
                        pltpu.VMEM((HEADS_PER_STEP, LANE, tq), F32),
                        pltpu.VMEM(stage + (tq, tq), F32),
                        pltpu.VMEM(stage + (tq, tq), F32),
                        pltpu.VMEM(stage + (tq // KV_TILE, 2 * KV_TILE, tq), BF16),
                        pltpu.VMEM(stage + (tq // KV_TILE, 1, tq), F32),
                        pltpu.VMEM(stage + (tq, tq), F32),
                        pltpu.VMEM(stage + (tq, tq), BF16)],
        compiler_params=_params("parallel", "parallel"),
        name="stick_prompt",
    )(q, k, vt)


def _band_prompt_kernel(q_ref, k_ref, vt_ref, base_ref, o_ref, bias_ref, s_ref, p_ref):
    b, qg = pl.program_id(1), pl.program_id(2)
    tq = BAND_Q_TILE
    n_sub = q_ref.shape[0] // tq

    @pl.when((b == 0) & (qg == 0))
    def _():
        i = lax.broadcasted_iota(jnp.int32, (tq, BAND_KEYS), 0) // CHUNK
        m = lax.broadcasted_iota(jnp.int32, (tq, BAND_KEYS), 1) // CHUNK
        in_band = (m >= i) & (m <= i + LEFT_CHUNKS)
        for j in range(HEADS_PER_STEP):
            rows = jnp.broadcast_to(base_ref[0, j:j + 1, :], (tq, BIAS_WIDTH))
            rolled = pltpu.roll(rows, 0, 1, stride=1, stride_axis=0)
            bias_ref[j] = jnp.where(in_band, rolled[:, :BAND_KEYS], NEG_INF).T

    def attend(tiles):
        maxima = []
        for sub, (first_tile, n_tiles) in enumerate(tiles):
            q_pad = _pair_operands(q_ref[sub * tq:(sub + 1) * tq, :])
            width = n_tiles * KV_TILE
            start = first_tile * KV_TILE
            if not isinstance(start, int):
                start = pl.multiple_of(start, KV_TILE)
            k = k_ref[pl.ds(start, width), :]
            for j in range(HEADS_PER_STEP):
                s = _dot_nt(k, q_pad[j]) + bias_ref[j, BAND_KEYS - width:, :]
                s_ref[sub * HEADS_PER_STEP + j, :width, :] = s
                maxima.append(jnp.max(s, axis=0, keepdims=True))
        for sub, (first_tile, n_tiles) in enumerate(tiles):
            width = n_tiles * KV_TILE
            for j in range(HEADS_PER_STEP):
                c = sub * HEADS_PER_STEP + j
                p_ref[c, :width, :] = jnp.exp(s_ref[c, :width, :] - maxima[c]).astype(BF16)
        for sub, (first_tile, n_tiles) in enumerate(tiles):
            width = n_tiles * KV_TILE
            per_head = [_pv_t(vt_ref, first_tile, p_ref[sub * HEADS_PER_STEP + j, :width, :], j, True)
                        for j in range(HEADS_PER_STEP)]
            o_ref[sub * tq:(sub + 1) * tq, :] = _merge_heads_t(per_head, True)

    n_lead = A_WINDOW // tq
    band_tiles = BAND_KEYS // KV_TILE
    n_lead_steps = -(-n_lead // n_sub)
    for g in range(n_lead_steps):
        lead = [(0, t + 1) if t < n_lead else (t - n_lead, band_tiles) for t in range(g * n_sub, (g + 1) * n_sub)]
        pl.when(qg == g)(functools.partial(attend, lead))

    @pl.when(qg >= n_lead_steps)
    def _():
        attend([(qg * n_sub + sub - n_lead, band_tiles) for sub in range(n_sub)])


def _band_prompt(q, k, vt, base, b, s):
    tq = BAND_Q_TILE * BAND_TILES_PER_STEP
    nq = s // tq
    q_spec = pl.BlockSpec((tq, LANE), lambda hp, bi, qi: (bi * nq + qi, hp))
    k_spec = pl.BlockSpec((s, LANE), lambda hp, bi, qi: (bi, hp))
    vt_spec = pl.BlockSpec((s // KV_TILE, LANE, KV_TILE), lambda hp, bi, qi: (bi, hp, 0))
    base_spec = pl.BlockSpec((1, HEADS_PER_STEP, BIAS_WIDTH), lambda hp, bi, qi: (hp, 0, 0))
    return pl.pallas_call(
        _band_prompt_kernel,
        grid=(N_HEAD_PAIRS, b, s // tq),
        in_specs=[q_spec, k_spec, vt_spec, base_spec],
        out_specs=q_spec,
        out_shape=jax.ShapeDtypeStruct(q.shape, BF16),
        scratch_shapes=[pltpu.VMEM((HEADS_PER_STEP, BAND_KEYS, BAND_Q_TILE), F32),
                        pltpu.VMEM((BAND_TILES_PER_STEP * HEADS_PER_STEP, BAND_KEYS, BAND_Q_TILE), F32),
                        pltpu.VMEM((BAND_TILES_PER_STEP * HEADS_PER_STEP, BAND_KEYS, BAND_Q_TILE), BF16)],
        compiler_params=_params("arbitrary", "arbitrary", "arbitrary"),
        name="band_prompt",
    )(q, k, vt, base)


def _rel_rows_desc(rel_table, hi, lo):
    top = jnp.broadcast_to(rel_table[-1], (max(0, hi - max(lo, REL_MAX + 1) + 1), rel_table.shape[1]))
    mid = rel_table[max(lo, -REL_MAX) + REL_MAX:min(hi, REL_MAX) + REL_MAX + 1][::-1]
    bot = jnp.broadcast_to(rel_table[0], (max(0, min(hi, -REL_MAX - 1) - lo + 1), rel_table.shape[1]))
    return jnp.concatenate([top, mid, bot], axis=0)


def _band_base(rel_table, q_rows, n_keys, width):
    offset = n_keys - q_rows
    desc = _rel_rows_desc(rel_table, offset + q_rows - 1, offset - n_keys + 1)
    pad = jnp.zeros((width - desc.shape[0], rel_table.shape[1]), rel_table.dtype)
    return jnp.concatenate([desc[q_rows - 1:], pad, desc[:q_rows - 1]], axis=0).T


def _sample_bias(rel_table, l, n_cache):
    n = n_cache + l
    desc = _rel_rows_desc(rel_table, n_cache + l - 1, -(l - 1))
    cols = jnp.stack([desc[l - 1 - i:l - 1 - i + n] for i in range(l)], axis=1)
    return jnp.swapaxes(cols, 1, 2).reshape(n, rel_table.shape[1] * l)


def _new_key_masks(l, strict):
    i = lax.broadcasted_iota(jnp.int32, (N_HEADS * l, l), 0) % l
    j = lax.broadcasted_iota(jnp.int32, (N_HEADS * l, l), 1)
    return (j < i) if strict else (j <= i)


def _sample_scores(q_ref, kn_ref, kt_ref):
    l = q_ref.shape[0]
    q, kn = q_ref[...], kn_ref[...]
    kt = kt_ref[0, 0].astype(BF16)
    cache, new = [], []
    for h in range(N_HEADS):
        hs = slice(h * HEAD_DIM, (h + 1) * HEAD_DIM)
        cache.append(_dot(q[:, hs], kt[hs, :]))
        new.append(_dot_nt(q[:, hs], kn[:, hs]))
    return jnp.concatenate(cache, axis=0), jnp.concatenate(new, axis=0), l


def _sample_output(w_cache, w_new, vt_ref, vn_ref, l):
    vt = vt_ref[0, 0].astype(BF16)
    vn = vn_ref[...]
    wc, wn = w_cache.astype(BF16), w_new.astype(BF16)
    outs = []
    for h in range(N_HEADS):
        hs = slice(h * HEAD_DIM, (h + 1) * HEAD_DIM)
        rows = slice(h * l, (h + 1) * l)
        outs.append(_dot_nt(wc[rows], vt[hs, :]) + _dot(wn[rows], vn[:, hs]))
    return jnp.concatenate(outs, axis=1).astype(BF16)


def _softmax_two(s_cache, s_new):
    m = jnp.maximum(jnp.max(s_cache, axis=1, keepdims=True), jnp.max(s_new, axis=1, keepdims=True))
    p_cache, p_new = jnp.exp(s_cache - m), jnp.exp(s_new - m)
    inv = 1.0 / (jnp.sum(p_cache, axis=1, keepdims=True) + jnp.sum(p_new, axis=1, keepdims=True))
    return p_cache * inv, p_new * inv


def _col_blocks(x):
    n = x.shape[1]
    return [x[:, c:min(c + KV_TILE, n)] for c in range(0, n, KV_TILE)]


def _prefix_sums_cols(blocks):
    upto = jnp.where(lax.broadcasted_iota(jnp.int32, (KV_TILE, KV_TILE), 0)
                     <= lax.broadcasted_iota(jnp.int32, (KV_TILE, KV_TILE), 1), 1.0, 0.0).astype(BF16)
    carry = jnp.zeros((blocks[0].shape[0], 1), F32)
    out = []
    for x in blocks:
        m = x.shape[1]
        c = sum(_dot(part, upto[:m, :m]) for part in _split3(x)) + carry
        out.append(c)
        carry = c[:, m - 1:]
    return out


def _later_sums_cols(blocks):
    after = jnp.where(lax.broadcasted_iota(jnp.int32, (KV_TILE, KV_TILE), 0)
                      > lax.broadcasted_iota(jnp.int32, (KV_TILE, KV_TILE), 1), 1.0, 0.0).astype(BF16)
    carry = jnp.zeros((blocks[0].shape[0], 1), F32)
    out = []
    for x in reversed(blocks):
        m = x.shape[1]
        inner = sum(_dot(part, after[:m, :m]) for part in _split2(x))
        out.append(inner + carry)
        carry = carry + (inner[:, :1] + x[:, :1])
    return out[::-1]


def _band_sample_kernel(q_ref, kn_ref, vn_ref, kt_ref, vt_ref, bias_c_ref, bias_n_ref, o_ref):
    s_cache, s_new, l = _sample_scores(q_ref, kn_ref, kt_ref)
    p_cache, p_new = _softmax_two(s_cache + bias_c_ref[...], s_new + bias_n_ref[...])
    o_ref[...] = _sample_output(p_cache, p_new, vt_ref, vn_ref, l)


def _fox_sample_kernel(q_ref, kn_ref, vn_ref, kt_ref, vt_ref, fc_ref, fn_ref, o_ref):
    s_cache, s_new, l = _sample_scores(q_ref, kn_ref, kt_ref)
    eh = lax.broadcasted_iota(jnp.int32, (N_HEADS * l, N_HEADS), 0) // l
    ec = lax.broadcasted_iota(jnp.int32, (N_HEADS * l, N_HEADS), 1)
    expand = jnp.where(eh == ec, 1.0, 0.0).astype(BF16)
    log_f = fc_ref[0, 0]
    spread_c = jnp.broadcast_to(log_f[:, None, :], (N_HEADS, l, log_f.shape[1])).reshape(N_HEADS * l, -1)
    spread_n = sum(_dot_nt(expand, part) for part in _split3(fn_ref[:, :N_HEADS]))
    cums = _prefix_sums_cols(_col_blocks(spread_c) + [spread_n])
    s_cache = s_cache - jnp.concatenate(cums[:-1], axis=1)
    s_new = jnp.where(_new_key_masks(l, strict=False), s_new - cums[-1], NEG_INF)
    p_cache, p_new = _softmax_two(s_cache, s_new)
    o_ref[...] = _sample_output(p_cache, p_new, vt_ref, vn_ref, l)


def _stick_sample_kernel(q_ref, kn_ref, vn_ref, kt_ref, vt_ref, o_ref):
    z_cache, z_new, l = _sample_scores(q_ref, kn_ref, kt_ref)
    earlier = _new_key_masks(l, strict=True)
    lb_cache, lb_new = _log_sigmoid(z_cache), _log_sigmoid(z_new)
    keep_new = jnp.where(earlier, lb_new - z_new, 0.0)
    tails = _later_sums_cols(_col_blocks(lb_cache - z_cache) + [keep_new])
    w_cache = jnp.exp(lb_cache + jnp.concatenate(tails[:-1], axis=1))
    w_new = jnp.where(earlier, jnp.exp(lb_new + tails[-1]), 0.0)
    o_ref[...] = _sample_output(w_cache, w_new, vt_ref, vn_ref, l)


def _cache_t(cache):
    n, nb, past, nh, hd = cache.shape
    return jnp.transpose(cache, (0, 1, 3, 4, 2)).reshape(n, nb, nh * hd, past)


def _sample_call(kernel, name, q, k_new, v_new, cache_kt, cache_vt, slot, extra=(), extra_specs=()):
    _, nb, d, past = cache_kt.shape
    l = q.shape[0] // nb
    new = pl.BlockSpec((l, d), lambda b: (b, 0))
    cache = pl.BlockSpec((1, 1, d, past), lambda b: (slot, b, 0, 0))
    return pl.pallas_call(
        kernel,
        grid=(nb,),
        in_specs=[new, new, new, cache, cache, *extra_specs],
        out_specs=new,
        out_shape=jax.ShapeDtypeStruct(q.shape, BF16),
        compiler_params=_params("parallel"),
        name=name,
    )(q, k_new, v_new, cache_kt, cache_vt, *extra)


def _cumsum_kernel(f_ref, o_ref):
    upto = jnp.where(lax.broadcasted_iota(jnp.int32, (KV_TILE, KV_TILE), 1)
                     <= lax.broadcasted_iota(jnp.int32, (KV_TILE, KV_TILE), 0), 1.0, 0.0).astype(BF16)
    carry = jnp.zeros((1, f_ref.shape[1]), F32)
    for t in range(f_ref.shape[0] // KV_TILE):
        rows = slice(t * KV_TILE, (t + 1) * KV_TILE)
        c = sum(_dot(upto, part) for part in _split3(f_ref[rows, :])) + carry
        o_ref[rows, :] = c
        carry = c[KV_TILE - 1:, :]


def _cumsum_rows(log_f, s):
    spec = pl.BlockSpec((s, log_f.shape[1]), lambda i: (i, 0))
    return pl.pallas_call(
        _cumsum_kernel, grid=(log_f.shape[0] // s,), in_specs=[spec], out_specs=spec,
        out_shape=jax.ShapeDtypeStruct(log_f.shape, F32),
        compiler_params=_params("parallel"), name="forget_cumsum",
    )(log_f)


def kernel(x_prompt, x_sample, cache_a_k, cache_a_v, cache_b_k, cache_b_v, cache_b_logf, cache_c_k, cache_c_v, norm_ffn1, ffn1_gate, ffn1_up, ffn1_down, norm_mix, norm_ffn2, ffn2_gate, ffn2_up, ffn2_down, a_w_qkv, a_w_o, a_rel_bias, b_w_qkv, b_w_o, b_w_f, b_b_f, c_w_qkv, c_w_o, norm_final):
    b, s, d = x_prompt.shape
    nb, l, _ = x_sample.shape
    depth = norm_ffn1.shape[0]
    assert d == N_HEADS * HEAD_DIM and s % CAUSAL_TILE == 0 and s >= A_WINDOW and A_WINDOW % BAND_Q_TILE == 0

    xp = x_prompt.reshape(b * s, d)
    xs = x_sample.reshape(nb * l, d)
    heads = (N_HEADS, HEAD_DIM)
    order = ("a_kp", "a_vp", "a_ks", "a_vs", "b_kp", "b_vp", "b_fp", "b_ks", "b_vs", "b_fs",
             "c_kp", "c_vp", "c_ks", "c_vs")
    outs = {name: [] for name in order}
    w1 = (ffn1_gate.astype(BF16), ffn1_up.astype(BF16), ffn1_down.astype(BF16))
    w2 = (ffn2_gate.astype(BF16), ffn2_up.astype(BF16), ffn2_down.astype(BF16))
    w_qkv = (a_w_qkv.astype(BF16), b_w_qkv.astype(BF16), c_w_qkv.astype(BF16))
    w_out = (a_w_o.astype(BF16), b_w_o.astype(BF16), c_w_o.astype(BF16))
    forget_w = (jnp.pad(b_w_f, ((0, 0), (0, 0), (0, LANE - N_HEADS))).astype(BF16),
                jnp.pad(b_b_f, ((0, 0), (0, LANE - N_HEADS))).reshape(-1, 1, LANE))
    assert min(A_WINDOW, s) == ROW_TILE
    caches_t = {name: _cache_t(c) for name, c in (("ak", cache_a_k), ("av", cache_a_v), ("bk", cache_b_k),
                                                  ("bv", cache_b_v), ("ck", cache_c_k), ("cv", cache_c_v))}

    def cache_order(t):
        return jnp.transpose(t.reshape(t.shape[0], N_HEADS, HEAD_DIM, t.shape[2]), (0, 3, 1, 2))

    for i in range(depth):
        kind, slot = i % 3, i // 3
        xp = _ffn(xp, i, norm_ffn1, *w1)
        xs = _ffn(xs, i, norm_ffn1, *w1)
        forget = forget_w if kind == 1 else None
        qp, kpt, vpt, kpb, vptb, *rest_p = _qkv(xp, i, slot, norm_mix, w_qkv[kind], forget, seq=s,
                                                keep_last=kind == 0)
        qs, ks, vs, ksb, vsb, *rest_s = _qkv(xs, i, slot, norm_mix, w_qkv[kind], forget)
        if kind == 0:
            base = _band_base(a_rel_bias[slot], BAND_Q_TILE, BAND_KEYS, BIAS_WIDTH)
            base = base.reshape(N_HEAD_PAIRS, HEADS_PER_STEP, BIAS_WIDTH)
            op = _band_prompt(qp, kpb, vptb, base, b, s)
            n_cache = cache_a_k.shape[2]
            bias_s = _sample_bias(a_rel_bias[slot], l, n_cache).T
            bias_c, bias_n = bias_s[:, :n_cache], bias_s[:, n_cache:]
            os_ = _sample_call(_band_sample_kernel, "band_sample", qs, ksb, vsb, caches_t["ak"], caches_t["av"],
                               slot, extra=(bias_c, bias_n),
                               extra_specs=(_resident(bias_c.shape), _resident(bias_n.shape)))
            outs["a_kp"].append(cache_order(rest_p[0]))
            outs["a_vp"].append(cache_order(rest_p[1]))
            outs["a_ks"].append(ks.reshape(nb, l, *heads))
            outs["a_vs"].append(vs.reshape(nb, l, *heads))
        elif kind == 1:
            fp, fs = rest_p[0], rest_s[0]
            op = _fox_prompt(qp, kpb, vptb, _cumsum_rows(fp, s), b, s)
            past = cache_b_k.shape[2]
            logf_t = jnp.swapaxes(cache_b_logf, 2, 3)
            os_ = _sample_call(_fox_sample_kernel, "fox_sample", qs, ksb, vsb, caches_t["bk"], caches_t["bv"],
                               slot, extra=(logf_t, fs),
                               extra_specs=(pl.BlockSpec((1, 1, N_HEADS, past), lambda bi: (slot, bi, 0, 0)),
                                            pl.BlockSpec((l, LANE), lambda bi: (bi, 0))))
            outs["b_kp"].append(cache_order(kpt))
            outs["b_vp"].append(cache_order(vpt))
            outs["b_fp"].append(fp[:, :N_HEADS].reshape(b, s, N_HEADS))
            outs["b_ks"].append(ks.reshape(nb, l, *heads))
            outs["b_vs"].append(vs.reshape(nb, l, *heads))
            outs["b_fs"].append(fs[:, :N_HEADS].reshape(nb, l, N_HEADS))
        else:
            op = _stick_prompt(qp, kpb, vptb, b, s)
            os_ = _sample_call(_stick_sample_kernel, "stick_sample", qs, ksb, vsb, caches_t["ck"], caches_t["cv"],
                               slot)
            outs["c_kp"].append(cache_order(kpt))
            outs["c_vp"].append(cache_order(vpt))
            outs["c_ks"].append(ks.reshape(nb, l, *heads))
            outs["c_vs"].append(vs.reshape(nb, l, *heads))
        final = norm_final if i == depth - 1 else None
        xp = _ffn(xp, i, norm_ffn2, *w2, proj=(op, w_out[kind], slot), final_g=final)
        xs = _ffn(xs, i, norm_ffn2, *w2, proj=(os_, w_out[kind], slot), final_g=final)

    return (xp.reshape(b, s, d), xs.reshape(nb, l, d)) + tuple(jnp.stack(outs[name]) for name in order)
```

```python
import functools

import jax
import jax.numpy as jnp
from jax import lax
from jax.experimental import pallas as pl
from jax.experimental.pallas import tpu as pltpu

F32 = jnp.float32
BF16 = jnp.bfloat16

N_HEADS = 16
HEAD_DIM = 64
CHUNK = 64
LEFT_CHUNKS = 8
A_WINDOW = LEFT_CHUNKS * CHUNK
REL_MAX = 256
RMS_EPS = 1e-6
ATTN_SCALE = HEAD_DIM ** -0.5
NEG_INF = -1e30

LANE = 128
HEADS_PER_STEP = LANE // HEAD_DIM
N_HEAD_PAIRS = N_HEADS // HEADS_PER_STEP
VMEM_LIMIT = 56 * 1024 * 1024

ROW_TILE = 512
FF_TILE = 256
KV_TILE = 256
CAUSAL_TILE = 512
BAND_Q_TILE = 256
BAND_TILES_PER_STEP = 8
BAND_KEYS = BAND_Q_TILE + A_WINDOW
BIAS_WIDTH = 1024


def _params(*sem):
    return pltpu.CompilerParams(dimension_semantics=sem, vmem_limit_bytes=VMEM_LIMIT)


def _rms(x, g):
    y = x * lax.rsqrt(jnp.mean(x * x, axis=-1, keepdims=True) + RMS_EPS)
    return y * g


def _log_sigmoid(z):
    return jnp.minimum(z, 0.0) - jnp.log(1.0 + jnp.exp(-jnp.abs(z)))


def _dot(a, b):
    return jnp.dot(a, b, preferred_element_type=F32)


def _dot_nt(a, b):
    return lax.dot_general(a, b, (((1,), (1,)), ((), ())), preferred_element_type=F32)


def _split2(x):
    hi = x.astype(BF16)
    lo = (x - hi.astype(F32)).astype(BF16)
    return hi, lo


def _split3(x):
    hi, lo = _split2(x)
    lo2 = (x - hi.astype(F32) - lo.astype(F32)).astype(BF16)
    return hi, lo, lo2


def _resident(shape):
    return pl.BlockSpec(shape, lambda *_: (0,) * len(shape), pipeline_mode=pl.Buffered(1))


def _resident_layer(shape, layer):
    return pl.BlockSpec((None,) + tuple(shape[1:]), lambda *_: (layer,) + (0,) * (len(shape) - 1),
                        pipeline_mode=pl.Buffered(1))


def _ffn_kernel(*refs, has_proj, has_final):
    refs = list(refs)
    x_ref = refs.pop(0)
    if has_proj:
        o_ref, wo_ref = refs.pop(0), refs.pop(0)
    g_ref, wg_ref, wu_ref, wd_ref = refs[:4]
    refs = refs[4:]
    if has_final:
        gf_ref = refs.pop(0)
    out_ref = refs.pop(0)

    x = x_ref[...]
    if has_proj:
        x = x + _dot(o_ref[...], wo_ref[...])
    h = _rms(x, g_ref[...]).astype(BF16)
    d_ff = wg_ref.shape[1]
    acc = jnp.zeros(x.shape, F32)
    for c in range(d_ff // FF_TILE):
        sl = slice(c * FF_TILE, (c + 1) * FF_TILE)
        gate = _dot(h, wg_ref[:, sl])
        up = _dot(h, wu_ref[:, sl])
        act = (gate * jax.nn.sigmoid(gate) * up).astype(BF16)
        acc = acc + _dot(act, wd_ref[sl, :])
    y = x + 0.5 * acc
    if has_final:
        y = _rms(y, gf_ref[...])
    out_ref[...] = y


def _ffn(x, layer, g, wg, wu, wd, proj=None, final_g=None):
    n, d = x.shape
    d_ff = wg.shape[2]
    tm = min(ROW_TILE, n)
    assert n % tm == 0 and d_ff % FF_TILE == 0
    row = pl.BlockSpec((tm, d), lambda i: (i, 0))
    args, specs = [x], [row]
    if proj is not None:
        o, w_o, slot = proj
        args += [o, w_o]
        specs += [row, _resident_layer(w_o.shape, slot)]
    g = g.reshape(g.shape[0], 1, d)
    args += [g, wg, wu, wd]
    specs += [_resident_layer(a.shape, layer) for a in (g, wg, wu, wd)]
    if final_g is not None:
        args.append(final_g.reshape(1, d))
        specs.append(_resident((1, d)))
    return pl.pallas_call(
        functools.partial(_ffn_kernel, has_proj=proj is not None, has_final=final_g is not None),
        grid=(n // tm,),
        in_specs=specs,
        out_specs=row,
        out_shape=jax.ShapeDtypeStruct((n, d), F32),
        compiler_params=_params("parallel"),
        name="ffn",
    )(*args)


def _qkv_kernel(*refs, has_forget, transposed, has_keep):
    x_ref, g_ref, w_ref = refs[:3]
    refs = list(refs[3:])
    if has_forget:
        wf_ref, bf_ref = refs[:2]
        refs = refs[2:]
    q_ref, k_ref, v_ref, kb_ref, vb_ref = refs[:5]
    refs = refs[5:]
    d = x_ref.shape[1]
    h = _rms(x_ref[...], g_ref[...]).astype(BF16)
    q_ref[...] = (_dot(h, w_ref[:, :d]) * ATTN_SCALE).astype(BF16)
    k = _dot(h, w_ref[:, d:2 * d])
    v = _dot(h, w_ref[:, 2 * d:])
    kb_ref[...] = k.astype(BF16)
    if transposed:
        kt, vt = k.T, v.T
        k_ref[0] = kt
        v_ref[0] = vt
        vtb = vt.astype(BF16)
        for t in range(vb_ref.shape[0]):
            vb_ref[t] = vtb[:, t * KV_TILE:(t + 1) * KV_TILE]
    else:
        k_ref[...] = k.reshape(k_ref.shape)
        v_ref[...] = v.reshape(v_ref.shape)
        vb_ref[...] = v.astype(BF16)
    if has_forget:
        refs.pop(0)[...] = _log_sigmoid(_dot(h, wf_ref[...]) + bf_ref[...])
    if has_keep:
        refs[0][...] = kt
        refs[1][...] = vt


def _qkv(x, layer, slot, g, w, forget=None, seq=None, keep_last=False):
    n, d = x.shape
    tm = min(ROW_TILE, n)
    assert n % tm == 0 and tm % KV_TILE == 0
    row = pl.BlockSpec((tm, d), lambda i: (i, 0))
    g = g.reshape(g.shape[0], 1, d)
    args = [x, g, w]
    specs = [row, _resident_layer(g.shape, layer), _resident_layer(w.shape, slot)]
    if seq is not None:
        assert seq % tm == 0 and n % seq == 0
        per = seq // tm
        kv_shape = jax.ShapeDtypeStruct((n // seq, d, seq), F32)
        kv_spec = pl.BlockSpec((1, d, tm), lambda i: (i // per, 0, i % per))
        vb_shape = jax.ShapeDtypeStruct((n // KV_TILE, d, KV_TILE), BF16)
        vb_spec = pl.BlockSpec((tm // KV_TILE, d, KV_TILE), lambda i: (i, 0, 0))
    else:
        kv_shape = jax.ShapeDtypeStruct((n, N_HEADS, HEAD_DIM), F32)
        kv_spec = pl.BlockSpec((tm, N_HEADS, HEAD_DIM), lambda i: (i, 0, 0))
        vb_shape, vb_spec = jax.ShapeDtypeStruct((n, d), BF16), row
    shapes = [jax.ShapeDtypeStruct((n, d), BF16), kv_shape, kv_shape, jax.ShapeDtypeStruct((n, d), BF16), vb_shape]
    out_specs = [row, kv_spec, kv_spec, row, vb_spec]
    if forget is not None:
        args += list(forget)
        specs += [_resident_layer(a.shape, slot) for a in forget]
        shapes.append(jax.ShapeDtypeStruct((n, LANE), F32))
        out_specs.append(pl.BlockSpec((tm, LANE), lambda i: (i, 0)))
    if keep_last:
        keep_spec = pl.BlockSpec((None, d, tm), lambda i: (i // per, 0, 0))
        shapes += [jax.ShapeDtypeStruct((n // seq, d, tm), F32)] * 2
        out_specs += [keep_spec, keep_spec]
    return pl.pallas_call(
        functools.partial(_qkv_kernel, has_forget=forget is not None, transposed=seq is not None,
                          has_keep=keep_last),
        grid=(n // tm,),
        in_specs=specs,
        out_specs=out_specs,
        out_shape=shapes,
        compiler_params=_params("arbitrary" if keep_last else "parallel"),
        name="qkv",
    )(*args)


def _pair_operands(q):
    lane_head = lax.broadcasted_iota(jnp.int32, q.shape, 1) // HEAD_DIM
    return [jnp.where(lane_head == j, q, jnp.zeros_like(q)) for j in range(HEADS_PER_STEP)]


def _vt_tile(vt_ref, t, j, with_ones):
    vt = vt_ref[t]
    if not with_ones:
        return vt
    row_head = lax.broadcasted_iota(jnp.int32, vt.shape, 0) // HEAD_DIM
    return jnp.where(row_head == j, vt, jnp.ones_like(vt))


def _pv_t(vt_ref, first_tile, p, j, with_ones):
    out = None
    for t in range(p.shape[0] // KV_TILE):
        part = _dot(_vt_tile(vt_ref, first_tile + t, j, with_ones), p[t * KV_TILE:(t + 1) * KV_TILE])
        out = part if out is None else out + part
    return out


def _merge_heads_t(per_head, normalise):
    parts = []
    for j, acc in enumerate(per_head):
        own = acc[j * HEAD_DIM:(j + 1) * HEAD_DIM]
        if normalise:
            other = (1 - j) * HEAD_DIM
            own = own / acc[other:other + 1]
        parts.append(own)
    return jnp.concatenate(parts, axis=0).T.astype(BF16)


def _causal_specs(s):
    seq = pl.BlockSpec((s, LANE), lambda b, hp: (b, hp))
    return seq, pl.BlockSpec((s // KV_TILE, LANE, KV_TILE), lambda b, hp: (b, hp, 0))


def _fox_prompt_kernel(q_ref, k_ref, vt_ref, cum_ref, o_ref, cumrep_ref, s_ref, p_ref):
    hp = pl.program_id(1)
    tq = CAUSAL_TILE
    n_kv = tq // KV_TILE
    c = cum_ref[...]
    lane = lax.broadcasted_iota(jnp.int32, c.shape, 1)
    for j in range(HEADS_PER_STEP):
        col = jnp.sum(jnp.where(lane == hp * HEADS_PER_STEP + j, c, 0.0), axis=1, keepdims=True)
        cumrep_ref[j] = jnp.broadcast_to(col, c.shape)
    tri = (lax.broadcasted_iota(jnp.int32, (KV_TILE, KV_TILE), 0)
           <= lax.broadcasted_iota(jnp.int32, (KV_TILE, KV_TILE), 1))
    for qi in range(q_ref.shape[0] // tq):
        slot = qi % 2
        q_rows = slice(qi * tq, (qi + 1) * tq)
        q_pad = _pair_operands(q_ref[q_rows, :])
        blocks = [(kb * tq, tq, 0) for kb in range(qi)]
        blocks += [(qi * tq + t * KV_TILE, KV_TILE, t * KV_TILE) for t in range(n_kv)]
        col_max = []
        for j in range(HEADS_PER_STEP):
            cmax = jnp.full((1, tq), NEG_INF, F32)
            for r0, nr, c0 in blocks:
                rows = slice(r0, r0 + nr)
                s = _dot_nt(k_ref[rows, :], q_pad[j][c0:])
                s = s - jnp.concatenate([cumrep_ref[j, rows, :]] * ((tq - c0) // LANE), axis=1)
                if r0 >= qi * tq:
                    masked = jnp.where(tri, s[:, :KV_TILE], NEG_INF)
                    s = masked if c0 + KV_TILE == tq else jnp.concatenate([masked, s[:, KV_TILE:]], axis=1)
                s_ref[slot, j, rows, c0:] = s
                m = jnp.maximum(cmax[:, c0:], jnp.max(s, axis=0, keepdims=True))
                cmax = m if c0 == 0 else jnp.concatenate([cmax[:, :c0], m], axis=1)
            col_max.append(cmax)
        for j in range(HEADS_PER_STEP):
            for r0, nr, c0 in blocks:
                rows = slice(r0, r0 + nr)
                p_ref[slot, j, rows, c0:] = jnp.exp(s_ref[slot, j, rows, c0:] - col_max[j][:, c0:]).astype(BF16)
        per_head = []
        for j in range(HEADS_PER_STEP):
            acc = None
            for r0, nr, c0 in blocks:
                part = _pv_t(vt_ref, r0 // KV_TILE, p_ref[slot, j, r0:r0 + nr, c0:], j, True)
                if c0:
                    part = jnp.concatenate([jnp.zeros((LANE, c0), F32), part], axis=1)
                acc = part if acc is None else acc + part
            per_head.append(acc)
        o_ref[q_rows, :] = _merge_heads_t(per_head, True)


def _fox_prompt(q, k, vt, cum, b, s):
    tq = CAUSAL_TILE
    seq, vt_spec = _causal_specs(s)
    cum_spec = pl.BlockSpec((s, LANE), lambda bi, hp: (bi, 0))
    return pl.pallas_call(
        _fox_prompt_kernel,
        grid=(b, N_HEAD_PAIRS),
        in_specs=[seq, seq, vt_spec, cum_spec],
        out_specs=seq,
        out_shape=jax.ShapeDtypeStruct(q.shape, BF16),
        scratch_shapes=[pltpu.VMEM((HEADS_PER_STEP, s, LANE), F32),
                        pltpu.VMEM((2, HEADS_PER_STEP, s, tq), F32),
                        pltpu.VMEM((2, HEADS_PER_STEP, s, tq), BF16)],
        compiler_params=_params("parallel", "parallel"),
        name="fox_prompt",
    )(q, k, vt, cum)


def _stick_prompt_kernel(q_ref, k_ref, vt_ref, o_ref, later_ref, acc_ref, z_ref, lb_ref, x_ref, first_ref,
                         tail_ref, w_ref):
    tq = CAUSAL_TILE
    n_kv = tq // KV_TILE
    tri = (lax.broadcasted_iota(jnp.int32, (KV_TILE, KV_TILE), 0)
           < lax.broadcasted_iota(jnp.int32, (KV_TILE, KV_TILE), 1))
    after = jnp.where(lax.broadcasted_iota(jnp.int32, (KV_TILE, KV_TILE), 1)
                      > lax.broadcasted_iota(jnp.int32, (KV_TILE, KV_TILE), 0), 1.0, 0.0).astype(BF16)

    def on_diagonal(x, fill):
        masked = jnp.where(tri, x[:, :KV_TILE], fill)
        return masked if x.shape[1] == KV_TILE else jnp.concatenate([masked, x[:, KV_TILE:]], axis=1)

    pair_no = 0
    for qi in range(q_ref.shape[0] // tq):
        q_rows = slice(qi * tq, (qi + 1) * tq)
        q_pad = _pair_operands(q_ref[q_rows, :])
        later_ref[...] = jnp.zeros(later_ref.shape, F32)
        acc_ref[...] = jnp.zeros(acc_ref.shape, F32)
        for kb in reversed(range(qi + 1)):
            slot = pair_no % 2
            pair_no += 1
            diagonal = kb == qi
            blocks = [(t, slice(t * KV_TILE, (t + 1) * KV_TILE), t * KV_TILE if diagonal else 0)
                      for t in range(n_kv)]
            for j in range(HEADS_PER_STEP):
                for t, rows, c0 in blocks:
                    k = k_ref[kb * tq + t * KV_TILE:kb * tq + (t + 1) * KV_TILE, :]
                    z_ref[slot, j, rows, c0:] = _dot_nt(k, q_pad[j][c0:])
            for j in range(HEADS_PER_STEP):
                for t, rows, c0 in blocks:
                    z = z_ref[slot, j, rows, c0:]
                    log_beta = _log_sigmoid(z)
                    log_keep = log_beta - z
                    if diagonal:
                        log_keep = on_diagonal(log_keep, 0.0)
                    lb_ref[slot, j, rows, c0:] = log_beta
                    x_ref[slot, j, t, :, c0:] = log_keep.astype(BF16)
                    first_ref[slot, j, t, :, c0:] = log_keep[:1]
            for j in range(HEADS_PER_STEP):
                later = later_ref[j]
                for t, rows, c0 in reversed(blocks):
                    inner = _dot(after, x_ref[slot, j, t, :, c0:])
                    tail_ref[slot, j, rows, c0:] = inner + later[:, c0:]
                    grown = later[:, c0:] + (inner[:1] + first_ref[slot, j, t, :, c0:])
                    later = grown if c0 == 0 else jnp.concatenate([later[:, :c0], grown], axis=1)
                later_ref[j] = later
            for j in range(HEADS_PER_STEP):
                for t, rows, c0 in blocks:
                    w = jnp.exp(lb_ref[slot, j, rows, c0:] + tail_ref[slot, j, rows, c0:])
                    if diagonal:
                        w = on_diagonal(w, 0.0)
                    w_ref[slot, j, rows, c0:] = w.astype(BF16)
            for j in range(HEADS_PER_STEP):
                for t, rows, c0 in blocks:
                    part = _dot(_vt_tile(vt_ref, kb * n_kv + t, j, False), w_ref[slot, j, rows, c0:])
                    acc_ref[j, :, c0:] += part
        o_ref[q_rows, :] = _merge_heads_t([acc_ref[j] for j in range(HEADS_PER_STEP)], False)


def _stick_prompt(q, k, vt, b, s):
    tq = CAUSAL_TILE
    seq, vt_spec = _causal_specs(s)
    stage = (2, HEADS_PER_STEP)
    return pl.pallas_call(
        _stick_prompt_kernel,
        grid=(b, N_HEAD_PAIRS),
        in_specs=[seq, seq, vt_spec],
        out_specs=seq,
        out_shape=jax.ShapeDtypeStruct(q.shape, BF16),
        scratch_shapes=[pltpu.VMEM((HEADS_PER_STEP, 1, tq), F32),
                        pltpu.VMEM((HEADS_PER_STEP, LANE, tq), F32),
                        pltpu.VMEM(stage + (tq, tq), F32),
                        pltpu.VMEM(stage + (tq, tq), F32),
                        pltpu.VMEM(stage + (tq // KV_TILE, KV_TILE, tq), BF16),
                        pltpu.VMEM(stage + (tq // KV_TILE, 1, tq), F32),
                        pltpu.VMEM(stage + (tq, tq), F32),
                        pltpu.VMEM(stage + (tq, tq), BF16)],
        compiler_params=_params("parallel", "parallel"),
        name="stick_prompt",
    )(q, k, vt)


def _band_prompt_kernel(q_ref, k_ref, vt_ref, base_ref, o_ref, bias_ref, s_ref, p_ref):
    b, qg = pl.program_id(1), pl.program_id(2)
    tq = BAND_Q_TILE
    n_sub = q_ref.shape[0] // tq

    @pl.when((b == 0) & (qg == 0))
    def _():
        i = lax.broadcasted_iota(jnp.int32, (tq, BAND_KEYS), 0) // CHUNK
        m = lax.broadcasted_iota(jnp.int32, (tq, BAND_KEYS), 1) // CHUNK
        in_band = (m >= i) & (m <= i + LEFT_CHUNKS)
        for j in range(HEADS_PER_STEP):
            rows = jnp.broadcast_to(base_ref[0, j:j + 1, :], (tq, BIAS_WIDTH))
            rolled = pltpu.roll(rows, 0, 1, stride=1, stride_axis=0)
            bias_ref[j] = jnp.where(in_band, rolled[:, :BAND_KEYS], NEG_INF).T

    def attend(tiles):
        maxima = []
        for sub, (first_tile, n_tiles) in enumerate(tiles):
            q_pad = _pair_operands(q_ref[sub * tq:(sub + 1) * tq, :])
            width = n_tiles * KV_TILE
            start = first_tile * KV_TILE
            if not isinstance(start, int):
                start = pl.multiple_of(start, KV_TILE)
            k = k_ref[pl.ds(start, width), :]
            for j in range(HEADS_PER_STEP):
                s = _dot_nt(k, q_pad[j]) + bias_ref[j, BAND_KEYS - width:, :]
                s_ref[sub * HEADS_PER_STEP + j, :width, :] = s
                maxima.append(jnp.max(s, axis=0, keepdims=True))
        for sub, (first_tile, n_tiles) in enumerate(tiles):
            width = n_tiles * KV_TILE
            for j in range(HEADS_PER_STEP):
                c = sub * HEADS_PER_STEP + j
                p_ref[c, :width, :] = jnp.exp(s_ref[c, :width, :] - maxima[c]).astype(BF16)
        for sub, (first_tile, n_tiles) in enumerate(tiles):
            width = n_tiles * KV_TILE
            per_head = [_pv_t(vt_ref, first_tile, p_ref[sub * HEADS_PER_STEP + j, :width, :], j, True)
                        for j in range(HEADS_PER_STEP)]
            o_ref[sub * tq:(sub + 1) * tq, :] = _merge_heads_t(per_head, True)

    n_lead = A_WINDOW // tq
    band_tiles = BAND_KEYS // KV_TILE
    n_lead_steps = -(-n_lead // n_sub)
    for g in range(n_lead_steps):
        lead = [(0, t + 1) if t < n_lead else (t - n_lead, band_tiles) for t in range(g * n_sub, (g + 1) * n_sub)]
        pl.when(qg == g)(functools.partial(attend, lead))

    @pl.when(qg >= n_lead_steps)
    def _():
        attend([(qg * n_sub + sub - n_lead, band_tiles) for sub in range(n_sub)])


def _band_prompt(q, k, vt, base, b, s):
    tq = BAND_Q_TILE * BAND_TILES_PER_STEP
    nq = s // tq
    q_spec = pl.BlockSpec((tq, LANE), lambda hp, bi, qi: (bi * nq + qi, hp))
    k_spec = pl.BlockSpec((s, LANE), lambda hp, bi, qi: (bi, hp))
    vt_spec = pl.BlockSpec((s // KV_TILE, LANE, KV_TILE), lambda hp, bi, qi: (bi, hp, 0))
    base_spec = pl.BlockSpec((1, HEADS_PER_STEP, BIAS_WIDTH), lambda hp, bi, qi: (hp, 0, 0))
    return pl.pallas_call(
        _band_prompt_kernel,
        grid=(N_HEAD_PAIRS, b, s // tq),
        in_specs=[q_spec, k_spec, vt_spec, base_spec],
        out_specs=q_spec,
        out_shape=jax.ShapeDtypeStruct(q.shape, BF16),
        scratch_shapes=[pltpu.VMEM((HEADS_PER_STEP, BAND_KEYS, BAND_Q_TILE), F32),
                        pltpu.VMEM((BAND_TILES_PER_STEP * HEADS_PER_STEP, BAND_KEYS, BAND_Q_TILE), F32),
                        pltpu.VMEM((BAND_TILES_PER_STEP * HEADS_PER_STEP, BAND_KEYS, BAND_Q_TILE), BF16)],
        compiler_params=_params("arbitrary", "arbitrary", "arbitrary"),
        name="band_prompt",
    )(q, k, vt, base)


def _rel_rows_desc(rel_table, hi, lo):
    top = jnp.broadcast_to(rel_table[-1], (max(0, hi - max(lo, REL_MAX + 1) + 1), rel_table.shape[1]))
    mid = rel_table[max(lo, -REL_MAX) + REL_MAX:min(hi, REL_MAX) + REL_MAX + 1][::-1]
    bot = jnp.broadcast_to(rel_table[0], (max(0, min(hi, -REL_MAX - 1) - lo + 1), rel_table.shape[1]))
    return jnp.concatenate([top, mid, bot], axis=0)


def _band_base(rel_table, q_rows, n_keys, width):
    offset = n_keys - q_rows
    desc = _rel_rows_desc(rel_table, offset + q_rows - 1, offset - n_keys + 1)
    pad = jnp.zeros((width - desc.shape[0], rel_table.shape[1]), rel_table.dtype)
    return jnp.concatenate([desc[q_rows - 1:], pad, desc[:q_rows - 1]], axis=0).T


def _sample_bias(rel_table, l, n_cache):
    n = n_cache + l
    desc = _rel_rows_desc(rel_table, n_cache + l - 1, -(l - 1))
    cols = jnp.stack([desc[l - 1 - i:l - 1 - i + n] for i in range(l)], axis=1)
    return jnp.swapaxes(cols, 1, 2).reshape(n, rel_table.shape[1] * l)


def _new_key_masks(l, strict):
    i = lax.broadcasted_iota(jnp.int32, (N_HEADS * l, l), 0) % l
    j = lax.broadcasted_iota(jnp.int32, (N_HEADS * l, l), 1)
    return (j < i) if strict else (j <= i)


def _sample_scores(q_ref, kn_ref, kt_ref):
    l = q_ref.shape[0]
    q, kn = q_ref[...], kn_ref[...]
    kt = kt_ref[0, 0].astype(BF16)
    cache, new = [], []
    for h in range(N_HEADS):
        hs = slice(h * HEAD_DIM, (h + 1) * HEAD_DIM)
        cache.append(_dot(q[:, hs], kt[hs, :]))
        new.append(_dot_nt(q[:, hs], kn[:, hs]))
    return jnp.concatenate(cache, axis=0), jnp.concatenate(new, axis=0), l


def _sample_output(w_cache, w_new, vt_ref, vn_ref, l):
    vt = vt_ref[0, 0].astype(BF16)
    vn = vn_ref[...]
    wc, wn = w_cache.astype(BF16), w_new.astype(BF16)
    outs = []
    for h in range(N_HEADS):
        hs = slice(h * HEAD_DIM, (h + 1) * HEAD_DIM)
        rows = slice(h * l, (h + 1) * l)
        outs.append(_dot_nt(wc[rows], vt[hs, :]) + _dot(wn[rows], vn[:, hs]))
    return jnp.concatenate(outs, axis=1).astype(BF16)


def _softmax_two(s_cache, s_new):
    m = jnp.maximum(jnp.max(s_cache, axis=1, keepdims=True), jnp.max(s_new, axis=1, keepdims=True))
    p_cache, p_new = jnp.exp(s_cache - m), jnp.exp(s_new - m)
    inv = 1.0 / (jnp.sum(p_cache, axis=1, keepdims=True) + jnp.sum(p_new, axis=1, keepdims=True))
    return p_cache * inv, p_new * inv


def _col_blocks(x):
    n = x.shape[1]
    return [x[:, c:min(c + KV_TILE, n)] for c in range(0, n, KV_TILE)]


def _prefix_sums_cols(blocks):
    upto = jnp.where(lax.broadcasted_iota(jnp.int32, (KV_TILE, KV_TILE), 0)
                     <= lax.broadcasted_iota(jnp.int32, (KV_TILE, KV_TILE), 1), 1.0, 0.0).astype(BF16)
    carry = jnp.zeros((blocks[0].shape[0], 1), F32)
    out = []
    for x in blocks:
        m = x.shape[1]
        c = sum(_dot(part, upto[:m, :m]) for part in _split3(x)) + carry
        out.append(c)
        carry = c[:, m - 1:]
    return out


def _later_sums_cols(blocks):
    after = jnp.where(lax.broadcasted_iota(jnp.int32, (KV_TILE, KV_TILE), 0)
                      > lax.broadcasted_iota(jnp.int32, (KV_TILE, KV_TILE), 1), 1.0, 0.0).astype(BF16)
    carry = jnp.zeros((blocks[0].shape[0], 1), F32)
    out = []
    for x in reversed(blocks):
        m = x.shape[1]
        inner = sum(_dot(part, after[:m, :m]) for part in _split2(x))
        out.append(inner + carry)
        carry = carry + (inner[:, :1] + x[:, :1])
    return out[::-1]


def _band_sample_kernel(q_ref, kn_ref, vn_ref, kt_ref, vt_ref, bias_c_ref, bias_n_ref, o_ref):
    s_cache, s_new, l = _sample_scores(q_ref, kn_ref, kt_ref)
    p_cache, p_new = _softmax_two(s_cache + bias_c_ref[...], s_new + bias_n_ref[...])
    o_ref[...] = _sample_output(p_cache, p_new, vt_ref, vn_ref, l)


def _fox_sample_kernel(q_ref, kn_ref, vn_ref, kt_ref, vt_ref, fc_ref, fn_ref, o_ref):
    s_cache, s_new, l = _sample_scores(q_ref, kn_ref, kt_ref)
    eh = lax.broadcasted_iota(jnp.int32, (N_HEADS * l, N_HEADS), 0) // l
    ec = lax.broadcasted_iota(jnp.int32, (N_HEADS * l, N_HEADS), 1)
    expand = jnp.where(eh == ec, 1.0, 0.0).astype(BF16)
    log_f = fc_ref[0, 0]
    spread_c = jnp.broadcast_to(log_f[:, None, :], (N_HEADS, l, log_f.shape[1])).reshape(N_HEADS * l, -1)
    spread_n = sum(_dot_nt(expand, part) for part in _split3(fn_ref[:, :N_HEADS]))
    cums = _prefix_sums_cols(_col_blocks(spread_c) + [spread_n])
    s_cache = s_cache - jnp.concatenate(cums[:-1], axis=1)
    s_new = jnp.where(_new_key_masks(l, strict=False), s_new - cums[-1], NEG_INF)
    p_cache, p_new = _softmax_two(s_cache, s_new)
    o_ref[...] = _sample_output(p_cache, p_new, vt_ref, vn_ref, l)


def _stick_sample_kernel(q_ref, kn_ref, vn_ref, kt_ref, vt_ref, o_ref):
    z_cache, z_new, l = _sample_scores(q_ref, kn_ref, kt_ref)
    earlier = _new_key_masks(l, strict=True)
    lb_cache, lb_new = _log_sigmoid(z_cache), _log_sigmoid(z_new)
    keep_new = jnp.where(earlier, lb_new - z_new, 0.0)
    tails = _later_sums_cols(_col_blocks(lb_cache - z_cache) + [keep_new])
    w_cache = jnp.exp(lb_cache + jnp.concatenate(tails[:-1], axis=1))
    w_new = jnp.where(earlier, jnp.exp(lb_new + tails[-1]), 0.0)
    o_ref[...] = _sample_output(w_cache, w_new, vt_ref, vn_ref, l)


def _cache_t(cache):
    n, nb, past, nh, hd = cache.shape
    return jnp.transpose(cache, (0, 1, 3, 4, 2)).reshape(n, nb, nh * hd, past)


def _sample_call(kernel, name, q, k_new, v_new, cache_kt, cache_vt, slot, extra=(), extra_specs=()):
    _, nb, d, past = cache_kt.shape
    l = q.shape[0] // nb
    new = pl.BlockSpec((l, d), lambda b: (b, 0))
    cache = pl.BlockSpec((1, 1, d, past), lambda b: (slot, b, 0, 0))
    return pl.pallas_call(
        kernel,
        grid=(nb,),
        in_specs=[new, new, new, cache, cache, *extra_specs],
        out_specs=new,
        out_shape=jax.ShapeDtypeStruct(q.shape, BF16),
        compiler_params=_params("parallel"),
        name=name,
    )(q, k_new, v_new, cache_kt, cache_vt, *extra)


def _cumsum_kernel(f_ref, o_ref):
    upto = jnp.where(lax.broadcasted_iota(jnp.int32, (KV_TILE, KV_TILE), 1)
                     <= lax.broadcasted_iota(jnp.int32, (KV_TILE, KV_TILE), 0), 1.0, 0.0).astype(BF16)
    carry = jnp.zeros((1, f_ref.shape[1]), F32)
    for t in range(f_ref.shape[0] // KV_TILE):
        rows = slice(t * KV_TILE, (t + 1) * KV_TILE)
        c = sum(_dot(upto, part) for part in _split3(f_ref[rows, :])) + carry
        o_ref[rows, :] = c
        carry = c[KV_TILE - 1:, :]


def _cumsum_rows(log_f, s):
    spec = pl.BlockSpec((s, log_f.shape[1]), lambda i: (i, 0))
    return pl.pallas_call(
        _cumsum_kernel, grid=(log_f.shape[0] // s,), in_specs=[spec], out_specs=spec,
        out_shape=jax.ShapeDtypeStruct(log_f.shape, F32),
        compiler_params=_params("parallel"), name="forget_cumsum",
    )(log_f)


def kernel(x_prompt, x_sample, cache_a_k, cache_a_v, cache_b_k, cache_b_v, cache_b_logf, cache_c_k, cache_c_v, norm_ffn1, ffn1_gate, ffn1_up, ffn1_down, norm_mix, norm_ffn2, ffn2_gate, ffn2_up, ffn2_down, a_w_qkv, a_w_o, a_rel_bias, b_w_qkv, b_w_o, b_w_f, b_b_f, c_w_qkv, c_w_o, norm_final):
    b, s, d = x_prompt.shape
    nb, l, _ = x_sample.shape
    depth = norm_ffn1.shape[0]
    assert d == N_HEADS * HEAD_DIM and s % CAUSAL_TILE == 0 and s >= A_WINDOW and A_WINDOW % BAND_Q_TILE == 0

    xp = x_prompt.reshape(b * s, d)
    xs = x_sample.reshape(nb * l, d)
    heads = (N_HEADS, HEAD_DIM)
    order = ("a_kp", "a_vp", "a_ks", "a_vs", "b_kp", "b_vp", "b_fp", "b_ks", "b_vs", "b_fs",
             "c_kp", "c_vp", "c_ks", "c_vs")
    outs = {name: [] for name in order}
    w1 = (ffn1_gate.astype(BF16), ffn1_up.astype(BF16), ffn1_down.astype(BF16))
    w2 = (ffn2_gate.astype(BF16), ffn2_up.astype(BF16), ffn2_down.astype(BF16))
    w_qkv = (a_w_qkv.astype(BF16), b_w_qkv.astype(BF16), c_w_qkv.astype(BF16))
    w_out = (a_w_o.astype(BF16), b_w_o.astype(BF16), c_w_o.astype(BF16))
    forget_w = (jnp.pad(b_w_f, ((0, 0), (0, 0), (0, LANE - N_HEADS))).astype(BF16),
                jnp.pad(b_b_f, ((0, 0), (0, LANE - N_HEADS))).reshape(-1, 1, LANE))
    assert min(A_WINDOW, s) == ROW_TILE
    caches_t = {name: _cache_t(c) for name, c in (("ak", cache_a_k), ("av", cache_a_v), ("bk", cache_b_k),
                                                  ("bv", cache_b_v), ("ck", cache_c_k), ("cv", cache_c_v))}

    def cache_order(t):
        return jnp.transpose(t.reshape(t.shape[0], N_HEADS, HEAD_DIM, t.shape[2]), (0, 3, 1, 2))

    for i in range(depth):
        kind, slot = i % 3, i // 3
        xp = _ffn(xp, i, norm_ffn1, *w1)
        xs = _ffn(xs, i, norm_ffn1, *w1)
        forget = forget_w if kind == 1 else None
        qp, kpt, vpt, kpb, vptb, *rest_p = _qkv(xp, i, slot, norm_mix, w_qkv[kind], forget, seq=s,
                                                keep_last=kind == 0)
        qs, ks, vs, ksb, vsb, *rest_s = _qkv(xs, i, slot, norm_mix, w_qkv[kind], forget)
        if kind == 0:
            base = _band_base(a_rel_bias[slot], BAND_Q_TILE, BAND_KEYS, BIAS_WIDTH)
            base = base.reshape(N_HEAD_PAIRS, HEADS_PER_STEP, BIAS_WIDTH)
            op = _band_prompt(qp, kpb, vptb, base, b, s)
            n_cache = cache_a_k.shape[2]
            bias_s = _sample_bias(a_rel_bias[slot], l, n_cache).T
            bias_c, bias_n = bias_s[:, :n_cache], bias_s[:, n_cache:]
            os_ = _sample_call(_band_sample_kernel, "band_sample", qs, ksb, vsb, caches_t["ak"], caches_t["av"],
                               slot, extra=(bias_c, bias_n),
                               extra_specs=(_resident(bias_c.shape), _resident(bias_n.shape)))
            outs["a_kp"].append(cache_order(rest_p[0]))
            outs["a_vp"].append(cache_order(rest_p[1]))
            outs["a_ks"].append(ks.reshape(nb, l, *heads))
            outs["a_vs"].append(vs.reshape(nb, l, *heads))
        elif kind == 1:
            fp, fs = rest_p[0], rest_s[0]
            op = _fox_prompt(qp, kpb, vptb, _cumsum_rows(fp, s), b, s)
            past = cache_b_k.shape[2]
            logf_t = jnp.swapaxes(cache_b_logf, 2, 3)
            os_ = _sample_call(_fox_sample_kernel, "fox_sample", qs, ksb, vsb, caches_t["bk"], caches_t["bv"],
                               slot, extra=(logf_t, fs),
                               extra_specs=(pl.BlockSpec((1, 1, N_HEADS, past), lambda bi: (slot, bi, 0, 0)),
                                            pl.BlockSpec((l, LANE), lambda bi: (bi, 0))))
            outs["b_kp"].append(cache_order(kpt))
            outs["b_vp"].append(cache_order(vpt))
            outs["b_fp"].append(fp[:, :N_HEADS].reshape(b, s, N_HEADS))
            outs["b_ks"].append(ks.reshape(nb, l, *heads))
            outs["b_vs"].append(vs.reshape(nb, l, *heads))
            outs["b_fs"].append(fs[:, :N_HEADS].reshape(nb, l, N_HEADS))
        else:
            op = _stick_prompt(qp, kpb, vptb, b, s)
            os_ = _sample_call(_stick_sample_kernel, "stick_sample", qs, ksb, vsb, caches_t["ck"], caches_t["cv"],
                               slot)
            outs["c_kp"].append(cache_order(kpt))
            outs["c_vp"].append(cache_order(vpt))
            outs["c_ks"].append(ks.reshape(nb, l, *heads))
            outs["c_vs"].append(vs.reshape(nb, l, *heads))
        final = norm_final if i == depth - 1 else None
        xp = _ffn(xp, i, norm_ffn2, *w2, proj=(op, w_out[kind], slot), final_g=final)
        xs = _ffn(xs, i, norm_ffn2, *w2, proj=(os_, w_out[kind], slot), final_g=final)

    return (xp.reshape(b, s, d), xs.reshape(nb, l, d)) + tuple(jnp.stack(outs[name]) for name in order)
```
